```python
import math
import jax, jax.numpy as jnp
from jax import lax
import numpy as np


D_MODEL = 1024
BATCH = 2
SEQ = 16384
DEPTH = 2

D_MIX = D_MODEL
HEAD_DIM = 64
W_A = D_MIX // 4
W_B = D_MIX // 4
W_C = D_MIX - W_A - W_B
N_HEADS_A = W_A // HEAD_DIM
N_Q_HEADS = W_C // HEAD_DIM
GQA_GROUP = 4
N_KV_HEADS = N_Q_HEADS // GQA_GROUP
KV_W = N_KV_HEADS * HEAD_DIM
CHUNK = 128
CONV_WIDTH = 31
CONV_PAD = CONV_WIDTH // 2
WINDOW = 128
BLOCK = 128
N_BUCKETS = 32
MAX_DISTANCE = 128
LN_EPS = 1e-5
NEG_INF = -1e30
DEEPNORM_ALPHA = (2 * DEPTH) ** 0.25
DEEPNORM_BETA = (8 * DEPTH) ** -0.25
SPLITS = (W_A, W_A, W_A,
          W_B, W_B, W_B,
          W_C, KV_W, KV_W, W_C)
D_IN = 3 * W_A + 3 * W_B + 2 * W_C + 2 * KV_W

kernel_name = 'hybrid_gmlp_conformer_swa_deepnorm'


def layer_norm(x, g, b):
    xf = x.astype(jnp.float32)
    mu = jnp.mean(xf, axis=-1, keepdims=True)
    var = jnp.mean(jnp.square(xf - mu), axis=-1, keepdims=True)
    y = (xf - mu) * lax.rsqrt(var + LN_EPS)
    return (y * g.astype(jnp.float32) + b.astype(jnp.float32)).astype(x.dtype)


def t5_bucket(rel):
    nb = N_BUCKETS // 2
    max_exact = nb // 2
    ret = jnp.where(rel > 0, nb, 0)
    n = jnp.abs(rel)
    nf = jnp.maximum(n, 1).astype(jnp.float32)
    large = max_exact + (jnp.log(nf / max_exact) / math.log(MAX_DISTANCE / max_exact)
                         * (nb - max_exact)).astype(jnp.int32)
    large = jnp.minimum(large, nb - 1)
    return ret + jnp.where(n < max_exact, n, large)


def band_geometry(seq):
    nblk = seq // BLOCK
    qq = jnp.arange(BLOCK)[:, None]
    kk = jnp.arange(3 * BLOCK)[None, :]
    rel = kk - BLOCK - qq
    in_window = jnp.abs(rel) <= WINDOW
    key_pos = jnp.arange(nblk)[:, None] * BLOCK - BLOCK + jnp.arange(3 * BLOCK)[None, :]
    valid = (key_pos >= 0) & (key_pos < seq)
    mask = in_window[None] & valid[:, None, :]
    return rel, mask


def mixer_spatial_gating(u, v, gate, ln_g, ln_b, w_s, b_s):
    bsz, seq, _ = u.shape
    u = jax.nn.gelu(u)
    v = layer_norm(jax.nn.gelu(v), ln_g, ln_b)
    v = v.reshape(bsz, seq // CHUNK, CHUNK, N_HEADS_A, HEAD_DIM)
    v = jnp.einsum('hij,bcjhd->bcihd', w_s, v) + b_s.T[None, None, :, :, None]
    return u * v.reshape(bsz, seq, W_A) * jax.nn.silu(gate)


def mixer_conformer_conv(a, b, gate, conv_w, conv_b, ln_g, ln_b):
    y = a * jax.nn.sigmoid(b)
    y = lax.conv_general_dilated(
        y, conv_w[:, None, :], window_strides=(1,), padding=[(CONV_PAD, CONV_PAD)],
        dimension_numbers=('NWC', 'WIO', 'NWC'), feature_group_count=W_B) + conv_b
    y = layer_norm(y, ln_g, ln_b)
    return jax.nn.silu(y) * jax.nn.silu(gate)


def mixer_window_attention(q, k, v, gate, sink, bias, mask):
    bsz, seq, _ = q.shape
    nblk = seq // BLOCK
    q = q.reshape(bsz, nblk, BLOCK, N_KV_HEADS, GQA_GROUP, HEAD_DIM)

    def band(t):
        t = t.reshape(bsz, seq, N_KV_HEADS, HEAD_DIM)
        t = jnp.pad(t, ((0, 0), (BLOCK, BLOCK), (0, 0), (0, 0)))
        t = t.reshape(bsz, nblk + 2, BLOCK, N_KV_HEADS, HEAD_DIM)
        return jnp.concatenate([t[:, :nblk], t[:, 1:nblk + 1], t[:, 2:]], axis=2)

    kb, vb = band(k), band(v)
    s = jnp.einsum('bnqkgd,bnskd->bnkgqs', q, kb).astype(jnp.float32) * (HEAD_DIM ** -0.5)
    s = jnp.where(mask[None, :, None, None], s + bias, NEG_INF)
    sk = sink.astype(jnp.float32).reshape(1, 1, N_KV_HEADS, GQA_GROUP, 1, 1)
    m = jnp.maximum(jnp.max(s, axis=-1, keepdims=True), sk)
    p = jnp.exp(s - m)
    p = p / (jnp.sum(p, axis=-1, keepdims=True) + jnp.exp(sk - m))
    o = jnp.einsum('bnkgqs,bnskd->bnqkgd', p.astype(vb.dtype), vb)
    return o.reshape(bsz, seq, W_C) * jax.nn.silu(gate)


def setup_inputs(seed: int = 0) -> dict:
    key = jax.random.key(seed)
    ks = jax.random.split(key, 20)
    f32 = jnp.float32
    nrm = lambda k, shape: jax.random.normal(k, shape, f32)
    return {
        'x': nrm(ks[0], (BATCH, SEQ, D_MODEL)),
        'ln_in_g': 1.0 + 0.05 * nrm(ks[1], (D_MODEL,)),
        'ln_in_b': 0.02 * nrm(ks[2], (D_MODEL,)),
        'w_in': nrm(ks[3], (DEPTH, D_MODEL, D_IN)) * D_MODEL ** -0.5,
        'gmlp_ln_g': 1.0 + 0.05 * nrm(ks[4], (DEPTH, W_A)),
        'gmlp_ln_b': 0.02 * nrm(ks[5], (DEPTH, W_A)),
        'w_spatial': nrm(ks[6], (DEPTH, N_HEADS_A, CHUNK, CHUNK)) * CHUNK ** -0.5,
        'b_spatial': 1.0 + 0.1 * nrm(ks[7], (DEPTH, N_HEADS_A, CHUNK)),
        'conv_w': nrm(ks[8], (DEPTH, CONV_WIDTH, W_B)) * CONV_WIDTH ** -0.5,
        'conv_b': 0.02 * nrm(ks[9], (DEPTH, W_B)),
        'conv_ln_g': 1.0 + 0.05 * nrm(ks[10], (DEPTH, W_B)),
        'conv_ln_b': 0.02 * nrm(ks[11], (DEPTH, W_B)),
        'attn_sink': 0.5 * nrm(ks[12], (DEPTH, N_Q_HEADS)),
        'rel_bias': 0.5 * nrm(ks[13], (N_BUCKETS, N_Q_HEADS)),
        'w_out': nrm(ks[14], (DEPTH, D_MIX, D_MODEL)) * (D_MIX ** -0.5) * DEEPNORM_BETA,
        'post_ln_g': 1.0 + 0.05 * nrm(ks[15], (DEPTH, D_MODEL)),
        'post_ln_b': 0.02 * nrm(ks[16], (DEPTH, D_MODEL)),
    }


def reference(x, ln_in_g, ln_in_b, w_in, gmlp_ln_g, gmlp_ln_b, w_spatial, b_spatial,
              conv_w, conv_b, conv_ln_g, conv_ln_b, attn_sink, rel_bias, w_out,
              post_ln_g, post_ln_b):
    seq = x.shape[1]
    split_points = np.cumsum(SPLITS)[:-1].tolist()
    rel, mask = band_geometry(seq)
    bias = rel_bias.astype(jnp.float32)[t5_bucket(rel)]
    bias = jnp.transpose(bias, (2, 0, 1)).reshape(N_KV_HEADS, GQA_GROUP, BLOCK, 3 * BLOCK)

    x = layer_norm(x, ln_in_g, ln_in_b)
    for l in range(DEPTH):
        h = jnp.einsum('bsd,de->bse', x, w_in[l])
        au, av, ag, ba, bb, bg, cq, ck, cv, cg = jnp.split(h, split_points, axis=-1)
        ya = mixer_spatial_gating(au, av, ag, gmlp_ln_g[l], gmlp_ln_b[l],
                                  w_spatial[l], b_spatial[l])
        yb = mixer_conformer_conv(ba, bb, bg, conv_w[l], conv_b[l],
                                  conv_ln_g[l], conv_ln_b[l])
        yc = mixer_window_attention(cq, ck, cv, cg, attn_sink[l], bias, mask)
        y = jnp.einsum('bse,ed->bsd', jnp.concatenate([ya, yb, yc], axis=-1), w_out[l])
        x = layer_norm(DEEPNORM_ALPHA * x + y, post_ln_g[l], post_ln_b[l])
    return x
```

```python
import functools
import math

import jax
import jax.numpy as jnp
from jax import lax
from jax.experimental import pallas as pl
from jax.experimental.pallas import tpu as pltpu

D_MODEL = 1024
HEAD_DIM = 64
W_A = 256
W_B = 256
W_C = 512
N_HEADS_A = 4
N_Q_HEADS = 8
N_KV_HEADS = 2
KV_W = N_KV_HEADS * HEAD_DIM
CHUNK = 128
CONV_WIDTH = 31
CONV_PAD = CONV_WIDTH // 2
BLOCK = 128
N_BUCKETS = 32
MAX_DISTANCE = 128
LN_EPS = 1e-5
NEG_INF = -1e30
DEPTH = 2
DEEPNORM_ALPHA = (2 * DEPTH) ** 0.25

COL_A = 0
COL_B = 3 * W_A
COL_Q = COL_B + 3 * W_B
COL_KV = COL_Q + W_C
COL_CG = COL_KV + 2 * KV_W
D_IN = COL_CG + W_C

LANES = 128
CONV_HALO = 16
CONV_ROWS = 64
VMEM_LIMIT_BYTES = 56 * 1024 * 1024

_BF16 = jnp.bfloat16
_F32 = jnp.float32


def _layer_norm(x, g, b):
    mu = jnp.mean(x, axis=-1, keepdims=True)
    xc = x - mu
    var = jnp.mean(xc * xc, axis=-1, keepdims=True)
    return xc * lax.rsqrt(var + LN_EPS) * g + b


def _dot(a, b):
    return jnp.dot(a, b, preferred_element_type=_F32)


def _bias_kernel(bucket_ref, rb_ref, out_ref):
    bucket = bucket_ref[...]
    row = lax.broadcasted_iota(jnp.int32, (BLOCK, 3 * BLOCK), 0)
    col = lax.broadcasted_iota(jnp.int32, (BLOCK, 3 * BLOCK), 1)
    in_window = jnp.abs(col - BLOCK - row) <= BLOCK
    for h in range(N_Q_HEADS):
        acc = jnp.zeros((BLOCK, 3 * BLOCK), _F32)
        for b in range(N_BUCKETS):
            acc = jnp.where(bucket == b, rb_ref[b, h], acc)
        base = jnp.where(in_window, acc, NEG_INF)
        out_ref[0, h] = base
        out_ref[1, h] = jnp.where(col < BLOCK, NEG_INF, base)
        out_ref[2, h] = jnp.where(col >= 2 * BLOCK, NEG_INF, base)


def _layer_kernel(first_layer, tm, nblk_seq,
                  xc_ref, xp_ref, xn_ref, ling_ref, linb_ref, win_ref,
                  gg_ref, gb_ref, wcat_ref, bsp_ref,
                  cw_ref, cb_ref, clg_ref, clb_ref,
                  sink_ref, biasm_ref, wout_ref, pg_ref, pb_ref,
                  out_ref,
                  xb_s, xres_s, ha_s, hb_s, qb_s, cg_s, kvar_s, ybuf_s, ymix_s, yout_s):
    i = pl.program_id(1)
    n_tiles = pl.num_programs(1)
    nblk = tm // BLOCK

    def normed(x):
        if first_layer:
            return _layer_norm(x, ling_ref[...], linb_ref[...])
        return x

    def prep_body(c, carry):
        rows = pl.ds(pl.multiple_of(c * BLOCK, BLOCK), BLOCK)
        xv = normed(xc_ref[rows, :])
        xres_s[rows, :] = xv
        xb_s[rows, :] = xv.astype(_BF16)
        return carry

    lax.fori_loop(0, nblk, prep_body, 0)
    xpb = normed(xp_ref[...]).astype(_BF16)
    xnb = normed(xn_ref[...]).astype(_BF16)

    xbv = xb_s[...]
    ha_s[...] = _dot(xbv, win_ref[:, COL_A:COL_B])
    hb_s[...] = _dot(xbv, win_ref[:, COL_B:COL_Q])
    qb_s[...] = (_dot(xbv, win_ref[:, COL_Q:COL_KV]) * (HEAD_DIM ** -0.5)).astype(_BF16)
    cg_s[...] = _dot(xbv, win_ref[:, COL_CG:D_IN])

    low = lax.broadcasted_iota(jnp.int32, (BLOCK, LANES), 1) < HEAD_DIM

    def store_variants(kv_blk, row0):
        rows = pl.ds(row0, BLOCK)
        for t in range(2):
            nat = kv_blk[:, t * KV_W:(t + 1) * KV_W]
            swp = pltpu.roll(nat, HEAD_DIM, axis=1)
            kvar_s[4 * t + 0, rows, :] = jnp.where(low, nat, 0.0).astype(_BF16)
            kvar_s[4 * t + 1, rows, :] = jnp.where(low, 0.0, swp).astype(_BF16)
            kvar_s[4 * t + 2, rows, :] = jnp.where(low, swp, 0.0).astype(_BF16)
            kvar_s[4 * t + 3, rows, :] = jnp.where(low, 0.0, nat).astype(_BF16)

    w_kv = win_ref[:, COL_KV:COL_CG]
    store_variants(_dot(xpb, w_kv), 0)
    store_variants(_dot(xnb, w_kv), BLOCK + tm)

    def kv_body(c, carry):
        r0 = pl.multiple_of(c * BLOCK, BLOCK)
        store_variants(_dot(xb_s[pl.ds(r0, BLOCK), :], w_kv), pl.multiple_of(r0 + BLOCK, BLOCK))
        return carry

    lax.fori_loop(0, nblk, kv_body, 0)

    head_of_lane = lax.broadcasted_iota(jnp.int32, (CHUNK, W_A), 1) // HEAD_DIM

    def gmlp_body(c, carry):
        rows = pl.ds(pl.multiple_of(c * CHUNK, CHUNK), CHUNK)
        u = jax.nn.gelu(ha_s[rows, 0:W_A])
        v = _layer_norm(jax.nn.gelu(ha_s[rows, W_A:2 * W_A]), gg_ref[...], gb_ref[...])
        gate = jax.nn.silu(ha_s[rows, 2 * W_A:3 * W_A])
        vb = v.astype(_BF16)
        zero = jnp.zeros_like(vb)
        rhs = jnp.concatenate([jnp.where(head_of_lane == h, vb, zero)
                               for h in range(N_HEADS_A)], axis=0)
        sp = _dot(wcat_ref[...], rhs) + bsp_ref[...]
        ymix_s[rows, 0:W_A] = (u * sp * gate).astype(_BF16)
        return carry

    lax.fori_loop(0, nblk, gmlp_body, 0)

    w_ab = win_ref[:, COL_B:COL_B + 2 * W_B]

    def glu(h):
        return h[:, 0:W_B] * jax.nn.sigmoid(h[:, W_B:2 * W_B])

    def glu_body(c, carry):
        r0 = pl.multiple_of(c * BLOCK, BLOCK)
        ybuf_s[pl.ds(r0 + CONV_HALO, BLOCK), :] = glu(hb_s[pl.ds(r0, BLOCK), 0:2 * W_B])
        return carry

    lax.fori_loop(0, nblk, glu_body, 0)
    yp = glu(_dot(xpb[BLOCK - CONV_HALO:BLOCK, :], w_ab))
    yn = glu(_dot(xnb[0:CONV_HALO, :], w_ab))
    ybuf_s[0:CONV_HALO, :] = jnp.where(i > 0, yp, 0.0)
    ybuf_s[CONV_HALO + tm:2 * CONV_HALO + tm, :] = jnp.where(i < n_tiles - 1, yn, 0.0)

    for ci in range(tm // CONV_ROWS):
        base = ci * CONV_ROWS
        acc = jnp.zeros((CONV_ROWS, W_B), _F32)
        for k in range(CONV_WIDTH):
            start = base + k + CONV_HALO - CONV_PAD
            acc = acc + ybuf_s[start:start + CONV_ROWS, :] * cw_ref[k:k + 1, :]
        z = _layer_norm(acc + cb_ref[...], clg_ref[...], clb_ref[...])
        gate = jax.nn.silu(hb_s[base:base + CONV_ROWS, 2 * W_B:3 * W_B])
        ymix_s[base:base + CONV_ROWS, W_A:W_A + W_B] = (jax.nn.silu(z) * gate).astype(_BF16)

    def attn_body(j, carry):
        r0 = pl.multiple_of(j * BLOCK, BLOCK)
        rows = pl.ds(r0, BLOCK)
        band = pl.ds(r0, 3 * BLOCK)
        gblk = i * nblk + j
        edge = jnp.where(gblk == 0, 1, jnp.where(gblk == nblk_seq - 1, 2, 0))
        for g in range(N_KV_HEADS):
            qg = qb_s[rows, g * 2 * LANES:(g + 1) * 2 * LANES]
            lhs = jnp.concatenate([qg[:, 0:LANES], qg[:, LANES:2 * LANES]], axis=0)
            kcat = jnp.concatenate([kvar_s[2 * g, band, :], kvar_s[2 * g + 1, band, :]], axis=0)
            s2 = lax.dot_general(lhs, kcat, (((1,), (1,)), ((), ())),
                                 preferred_element_type=_F32)
            p_rows = []
            r_inv = []
            for pp in range(2):
                p_pair = []
                for hh in range(2):
                    h = 4 * g + 2 * pp + hh
                    s = (s2[pp * BLOCK:(pp + 1) * BLOCK, hh * 3 * BLOCK:(hh + 1) * 3 * BLOCK]
                         + biasm_ref[edge, h])
                    sk = sink_ref[h]
                    m = jnp.maximum(jnp.max(s, axis=-1, keepdims=True), sk)
                    p = jnp.exp(s - m)
                    den = jnp.sum(p, axis=-1, keepdims=True) + jnp.exp(sk - m)
                    r_inv.append(1.0 / den)
                    p_pair.append(p.astype(_BF16))
                p_rows.append(jnp.concatenate(p_pair, axis=1))
            pcat = jnp.concatenate(p_rows, axis=0)
            vcat = jnp.concatenate([kvar_s[4 + 2 * g, band, :], kvar_s[5 + 2 * g, band, :]], axis=0)
            o2 = _dot(pcat, vcat)
            for pp in range(2):
                cols = slice((2 * g + pp) * LANES, (2 * g + pp + 1) * LANES)
                scale = jnp.where(low, r_inv[2 * pp], r_inv[2 * pp + 1])
                o = o2[pp * BLOCK:(pp + 1) * BLOCK, :] * scale
                gate = jax.nn.silu(cg_s[rows, cols])
                ymix_s[rows, W_A + W_B + cols.start:W_A + W_B + cols.stop] = (o * gate).astype(_BF16)
        return carry

    lax.fori_loop(0, nblk, attn_body, 0)

    yout_s[...] = _dot(ymix_s[...], wout_ref[...])

    def out_body(c, carry):
        rows = pl.ds(pl.multiple_of(c * BLOCK, BLOCK), BLOCK)
        z = DEEPNORM_ALPHA * xres_s[rows, :] + yout_s[rows, :]
        out_ref[rows, :] = _layer_norm(z, pg_ref[...], pb_ref[...])
        return carry

    lax.fori_loop(0, nblk, out_body, 0)


def _t5_bucket(rel):
    nb = N_BUCKETS // 2
    max_exact = nb // 2
    ret = jnp.where(rel > 0, nb, 0)
    n = jnp.abs(rel)
    nf = jnp.maximum(n, 1).astype(jnp.float32)
    large = max_exact + (jnp.log(nf / max_exact) / math.log(MAX_DISTANCE / max_exact)
                         * (nb - max_exact)).astype(jnp.int32)
    large = jnp.minimum(large, nb - 1)
    return ret + jnp.where(n < max_exact, n, large)


def _const_spec(shape):
    zeros = (0,) * len(shape)
    return pl.BlockSpec(shape, lambda b, i: zeros, pipeline_mode=pl.Buffered(1))


def _tile_rows(seq):
    tm = 512
    assert seq % tm == 0 and seq // BLOCK >= 2
    return tm


def _layer_call(first_layer, x, ling, linb, win, gg, gb, wcat, bsp, cw, cb, clg, clb,
                sink, biasm, wout, pg, pb):
    bsz, seq, d = x.shape
    tm = _tile_rows(seq)
    nb_tile = tm // BLOCK
    nblk_seq = seq // BLOCK
    grid = (bsz, seq // tm)

    in_specs = [
        pl.BlockSpec((None, tm, d), lambda b, i: (b, i, 0)),
        pl.BlockSpec((None, BLOCK, d), lambda b, i: (b, jnp.maximum(i * nb_tile - 1, 0), 0)),
        pl.BlockSpec((None, BLOCK, d),
                     lambda b, i: (b, jnp.minimum((i + 1) * nb_tile, nblk_seq - 1), 0)),
        _const_spec(ling.shape), _const_spec(linb.shape), _const_spec(win.shape),
        _const_spec(gg.shape), _const_spec(gb.shape), _const_spec(wcat.shape),
        _const_spec(bsp.shape),
        _const_spec(cw.shape), _const_spec(cb.shape), _const_spec(clg.shape),
        _const_spec(clb.shape),
        pl.BlockSpec(memory_space=pltpu.SMEM),
        _const_spec(biasm.shape), _const_spec(wout.shape),
        _const_spec(pg.shape), _const_spec(pb.shape),
    ]
    scratch = [
        pltpu.VMEM((tm, d), _BF16),
        pltpu.VMEM((tm, d), _F32),
        pltpu.VMEM((tm, 3 * W_A), _F32),
        pltpu.VMEM((tm, 3 * W_B), _F32),
        pltpu.VMEM((tm, W_C), _BF16),
        pltpu.VMEM((tm, W_C), _F32),
        pltpu.VMEM((8, tm + 2 * BLOCK, LANES), _BF16),
        pltpu.VMEM((tm + 2 * CONV_HALO, W_B), _F32),
        pltpu.VMEM((tm, d), _BF16),
        pltpu.VMEM((tm, d), _F32),
    ]
    return pl.pallas_call(
        functools.partial(_layer_kernel, first_layer, tm, nblk_seq),
        grid=grid,
        in_specs=in_specs,
        out_specs=pl.BlockSpec((None, tm, d), lambda b, i: (b, i, 0)),
        out_shape=jax.ShapeDtypeStruct(x.shape, x.dtype),
        scratch_shapes=scratch,
        compiler_params=pltpu.CompilerParams(
            dimension_semantics=("arbitrary", "arbitrary"),
            vmem_limit_bytes=VMEM_LIMIT_BYTES),
        name="layer_first" if first_layer else "layer_next",
    )(x, x, x, ling, linb, win, gg, gb, wcat, bsp, cw, cb, clg, clb, sink, biasm, wout, pg, pb)


def kernel(x, ln_in_g, ln_in_b, w_in, gmlp_ln_g, gmlp_ln_b, w_spatial, b_spatial, conv_w, conv_b,
           conv_ln_g, conv_ln_b, attn_sink, rel_bias, w_out, post_ln_g, post_ln_b):
    depth = w_in.shape[0]
    assert depth == DEPTH and x.shape[2] == D_MODEL and w_in.shape[2] == D_IN

    qq = jnp.arange(BLOCK)[:, None]
    kk = jnp.arange(3 * BLOCK)[None, :]
    bucket = _t5_bucket(kk - BLOCK - qq).astype(jnp.int32)
    biasm = pl.pallas_call(
        _bias_kernel,
        in_specs=[pl.BlockSpec(memory_space=pltpu.VMEM), pl.BlockSpec(memory_space=pltpu.SMEM)],
        out_specs=pl.BlockSpec(memory_space=pltpu.VMEM),
        out_shape=jax.ShapeDtypeStruct((3, N_Q_HEADS, BLOCK, 3 * BLOCK), _F32),
        name="rel_bias_tables",
    )(bucket, rel_bias.astype(_F32))

    row = lambda a: a.reshape(1, -1).astype(_F32)
    for l in range(depth):
        wcat = jnp.transpose(w_spatial[l], (1, 0, 2)).reshape(CHUNK, N_HEADS_A * CHUNK).astype(_BF16)
        bsp = jnp.repeat(b_spatial[l].T, HEAD_DIM, axis=1).astype(_F32)
        x = _layer_call(
            l == 0, x, row(ln_in_g), row(ln_in_b), w_in[l].astype(_BF16),
            row(gmlp_ln_g[l]), row(gmlp_ln_b[l]), wcat, bsp,
            conv_w[l].astype(_F32), row(conv_b[l]), row(conv_ln_g[l]), row(conv_ln_b[l]),
            attn_sink[l].astype(_F32), biasm, w_out[l].astype(_BF16),
            row(post_ln_g[l]), row(post_ln_b[l]))
    return x
```

```python
import functools
import math

import jax
import jax.numpy as jnp
from jax import lax
from jax.experimental import pallas as pl
from jax.experimental.pallas import tpu as pltpu

D_MODEL = 1024
HEAD_DIM = 64
W_A = 256
W_B = 256
W_C = 512
N_HEADS_A = 4
N_Q_HEADS = 8
N_KV_HEADS = 2
KV_W = N_KV_HEADS * HEAD_DIM
CHUNK = 128
CONV_WIDTH = 31
CONV_PAD = CONV_WIDTH // 2
BLOCK = 128
N_BUCKETS = 32
MAX_DISTANCE = 128
LN_EPS = 1e-5
NEG_INF = -1e30
DEPTH = 2
DEEPNORM_ALPHA = (2 * DEPTH) ** 0.25

COL_A = 0
COL_B = 3 * W_A
COL_Q = COL_B + 3 * W_B
COL_KV = COL_Q + W_C
COL_CG = COL_KV + 2 * KV_W
D_IN = COL_CG + W_C

LANES = 128
SUBLANES = 8
CONV_HALO = 16
CONV_ROWS = 64
VMEM_LIMIT_BYTES = 56 * 1024 * 1024

_BF16 = jnp.bfloat16
_F32 = jnp.float32


def _layer_norm(x, g, b):
    mu = jnp.mean(x, axis=-1, keepdims=True)
    xc = x - mu
    var = jnp.mean(xc * xc, axis=-1, keepdims=True)
    return xc * lax.rsqrt(var + LN_EPS) * g + b


def _dot(a, b):
    return jnp.dot(a, b, preferred_element_type=_F32)


def _bias_kernel(bucket_ref, rb_ref, out_ref):
    bucket = bucket_ref[...]
    row = lax.broadcasted_iota(jnp.int32, (BLOCK, 3 * BLOCK), 0)
    col = lax.broadcasted_iota(jnp.int32, (BLOCK, 3 * BLOCK), 1)
    in_window = jnp.abs(col - BLOCK - row) <= BLOCK
    for h in range(N_Q_HEADS):
        acc = jnp.zeros((BLOCK, 3 * BLOCK), _F32)
        for b in range(N_BUCKETS):
            acc = jnp.where(bucket == b, rb_ref[b, h], acc)
        base = jnp.where(in_window, acc, NEG_INF)
        out_ref[0, h] = base
        out_ref[1, h] = jnp.where(col < BLOCK, NEG_INF, base)
        out_ref[2, h] = jnp.where(col >= 2 * BLOCK, NEG_INF, base)


def _layer_kernel(first_layer, tm, nblk_seq,
                  xc_ref, xp_ref, xn_ref, ling_ref, linb_ref, win_ref,
                  gg_ref, gb_ref, wcat_ref, bsp_ref,
                  cw_ref, cb_ref, clg_ref, clb_ref,
                  sink_ref, biasm_ref, wout_ref, pg_ref, pb_ref,
                  out_ref,
                  xb_s, xres_s, ha_s, hb_s, qb_s, cg_s, kvf_s, kvar_s, ybuf_s, ymix_s, yout_s):
    i = pl.program_id(1)
    n_tiles = pl.num_programs(1)
    nblk = tm // BLOCK

    def normed(x):
        if first_layer:
            return _layer_norm(x, ling_ref[...], linb_ref[...])
        return x

    def prep_body(c, carry):
        rows = pl.ds(pl.multiple_of(c * BLOCK, BLOCK), BLOCK)
        xv = normed(xc_ref[rows, :])
        xres_s[rows, :] = xv
        xb_s[rows, :] = xv.astype(_BF16)
        return carry

    lax.fori_loop(0, nblk, prep_body, 0)
    xpb = normed(xp_ref[...]).astype(_BF16)
    xnb = normed(xn_ref[...]).astype(_BF16)

    xbv = xb_s[...]
    ha_s[...] = _dot(xbv, win_ref[:, COL_A:COL_B])
    hb_s[...] = _dot(xbv, win_ref[:, COL_B:COL_Q])
    qb_s[...] = (_dot(xbv, win_ref[:, COL_Q:COL_KV]) * (HEAD_DIM ** -0.5)).astype(_BF16)
    cg_s[...] = _dot(xbv, win_ref[:, COL_CG:D_IN])

    low = lax.broadcasted_iota(jnp.int32, (BLOCK, LANES), 1) < HEAD_DIM

    def store_variants(kv_blk, row0):
        rows = pl.ds(row0, BLOCK)
        for t in range(2):
            nat = kv_blk[:, t * KV_W:(t + 1) * KV_W]
            swp = pltpu.roll(nat, HEAD_DIM, axis=1)
            kvar_s[4 * t + 0, rows, :] = jnp.where(low, nat, 0.0).astype(_BF16)
            kvar_s[4 * t + 1, rows, :] = jnp.where(low, 0.0, swp).astype(_BF16)
            kvar_s[4 * t + 2, rows, :] = jnp.where(low, swp, 0.0).astype(_BF16)
            kvar_s[4 * t + 3, rows, :] = jnp.where(low, 0.0, nat).astype(_BF16)

    w_kv = win_ref[:, COL_KV:COL_CG]
    kvf_s[...] = _dot(xbv, w_kv)
    store_variants(_dot(xpb, w_kv), 0)
    store_variants(_dot(xnb, w_kv), BLOCK + tm)

    def kv_body(c, carry):
        r0 = pl.multiple_of(c * BLOCK, BLOCK)
        store_variants(kvf_s[pl.ds(r0, BLOCK), :], pl.multiple_of(r0 + BLOCK, BLOCK))
        return carry

    lax.fori_loop(0, nblk, kv_body, 0)

    head_of_lane = lax.broadcasted_iota(jnp.int32, (CHUNK, W_A), 1) // HEAD_DIM

    def gmlp_body(c, carry):
        rows = pl.ds(pl.multiple_of(c * CHUNK, CHUNK), CHUNK)
        u = jax.nn.gelu(ha_s[rows, 0:W_A])
        v = _layer_norm(jax.nn.gelu(ha_s[rows, W_A:2 * W_A]), gg_ref[...], gb_ref[...])
        gate = jax.nn.silu(ha_s[rows, 2 * W_A:3 * W_A])
        vb = v.astype(_BF16)
        zero = jnp.zeros_like(vb)
        rhs = jnp.concatenate([jnp.where(head_of_lane == h, vb, zero)
                               for h in range(N_HEADS_A)], axis=0)
        sp = _dot(wcat_ref[...], rhs) + bsp_ref[...]
        ymix_s[rows, 0:W_A] = (u * sp * gate).astype(_BF16)
        return carry

    lax.fori_loop(0, nblk, gmlp_body, 0)

    w_ab = win_ref[:, COL_B:COL_B + 2 * W_B]

    def glu(h):
        return h[:, 0:W_B] * jax.nn.sigmoid(h[:, W_B:2 * W_B])

    def glu_body(c, carry):
        r0 = pl.multiple_of(c * BLOCK, BLOCK)
        ybuf_s[pl.ds(r0 + CONV_HALO, BLOCK), :] = glu(hb_s[pl.ds(r0, BLOCK), 0:2 * W_B])
        return carry

    lax.fori_loop(0, nblk, glu_body, 0)
    yp = glu(_dot(xpb[BLOCK - CONV_HALO:BLOCK, :], w_ab))
    yn = glu(_dot(xnb[0:CONV_HALO, :], w_ab))
    ybuf_s[0:CONV_HALO, :] = jnp.where(i > 0, yp, 0.0)
    ybuf_s[CONV_HALO + tm:2 * CONV_HALO + tm, :] = jnp.where(i < n_tiles - 1, yn, 0.0)

    span = CONV_ROWS + 2 * CONV_HALO
    for ci in range(tm // CONV_ROWS):
        base = ci * CONV_ROWS
        ych = ybuf_s[base:base + span, :]
        acc = jnp.zeros((CONV_ROWS, W_B), _F32)
        for r in range(SUBLANES):
            zr = ych if r == 0 else pltpu.roll(ych, span - r, axis=0)
            for m in range(2 * CONV_HALO // SUBLANES):
                k = SUBLANES * m + r - 1
                if 0 <= k < CONV_WIDTH:
                    acc = acc + zr[SUBLANES * m:SUBLANES * m + CONV_ROWS, :] * cw_ref[k:k + 1, :]
        z = _layer_norm(acc + cb_ref[...], clg_ref[...], clb_ref[...])
        gate = jax.nn.silu(hb_s[base:base + CONV_ROWS, 2 * W_B:3 * W_B])
        ymix_s[base:base + CONV_ROWS, W_A:W_A + W_B] = (jax.nn.silu(z) * gate).astype(_BF16)

    def attn_body(j, carry):
        r0 = pl.multiple_of(j * BLOCK, BLOCK)
        rows = pl.ds(r0, BLOCK)
        band = pl.ds(r0, 3 * BLOCK)
        gblk = i * nblk + j
        edge = jnp.where(gblk == 0, 1, jnp.where(gblk == nblk_seq - 1, 2, 0))
        for g in range(N_KV_HEADS):
            qg = qb_s[rows, g * 2 * LANES:(g + 1) * 2 * LANES]
            lhs = jnp.concatenate([qg[:, 0:LANES], qg[:, LANES:2 * LANES]], axis=0)
            kcat = jnp.concatenate([kvar_s[2 * g, band, :], kvar_s[2 * g + 1, band, :]], axis=0)
            s2 = lax.dot_general(lhs, kcat, (((1,), (1,)), ((), ())),
                                 preferred_element_type=_F32)
            p_rows = []
            r_inv = []
            for pp in range(2):
                p_pair = []
                for hh in range(2):
                    h = 4 * g + 2 * pp + hh
                    s = (s2[pp * BLOCK:(pp + 1) * BLOCK, hh * 3 * BLOCK:(hh + 1) * 3 * BLOCK]
                         + biasm_ref[edge, h])
                    sk = sink_ref[h]
                    m = jnp.maximum(jnp.max(s, axis=-1, keepdims=True), sk)
                    p = jnp.exp(s - m)
                    den = jnp.sum(p, axis=-1, keepdims=True) + jnp.exp(sk - m)
                    r_inv.append(1.0 / den)
                    p_pair.append(p.astype(_BF16))
                p_rows.append(jnp.concatenate(p_pair, axis=1))
            pcat = jnp.concatenate(p_rows, axis=0)
            vcat = jnp.concatenate([kvar_s[4 + 2 * g, band, :], kvar_s[5 + 2 * g, band, :]], axis=0)
            o2 = _dot(pcat, vcat)
            for pp in range(2):
                cols = slice((2 * g + pp) * LANES, (2 * g + pp + 1) * LANES)
                scale = jnp.where(low, r_inv[2 * pp], r_inv[2 * pp + 1])
                o = o2[pp * BLOCK:(pp + 1) * BLOCK, :] * scale
                gate = jax.nn.silu(cg_s[rows, cols])
                ymix_s[rows, W_A + W_B + cols.start:W_A + W_B + cols.stop] = (o * gate).astype(_BF16)
        return carry

    lax.fori_loop(0, nblk, attn_body, 0)

    yout_s[...] = _dot(ymix_s[...], wout_ref[...])

    def out_body(c, carry):
        rows = pl.ds(pl.multiple_of(c * BLOCK, BLOCK), BLOCK)
        z = DEEPNORM_ALPHA * xres_s[rows, :] + yout_s[rows, :]
        out_ref[rows, :] = _layer_norm(z, pg_ref[...], pb_ref[...])
        return carry

    lax.fori_loop(0, nblk, out_body, 0)


def _t5_bucket(rel):
    nb = N_BUCKETS // 2
    max_exact = nb // 2
    ret = jnp.where(rel > 0, nb, 0)
    n = jnp.abs(rel)
    nf = jnp.maximum(n, 1).astype(jnp.float32)
    large = max_exact + (jnp.log(nf / max_exact) / math.log(MAX_DISTANCE / max_exact)
                         * (nb - max_exact)).astype(jnp.int32)
    large = jnp.minimum(large, nb - 1)
    return ret + jnp.where(n < max_exact, n, large)


def _const_spec(shape):
    zeros = (0,) * len(shape)
    return pl.BlockSpec(shape, lambda b, i: zeros, pipeline_mode=pl.Buffered(1))


def _tile_rows(seq):
    tm = 512
    assert seq % tm == 0 and seq // BLOCK >= 2
    return tm


def _layer_call(first_layer, x, ling, linb, win, gg, gb, wcat, bsp, cw, cb, clg, clb,
                sink, biasm, wout, pg, pb):
    bsz, seq, d = x.shape
    tm = _tile_rows(seq)
    nb_tile = tm // BLOCK
    nblk_seq = seq // BLOCK
    grid = (bsz, seq // tm)

    in_specs = [
        pl.BlockSpec((None, tm, d), lambda b, i: (b, i, 0)),
        pl.BlockSpec((None, BLOCK, d), lambda b, i: (b, jnp.maximum(i * nb_tile - 1, 0), 0)),
        pl.BlockSpec((None, BLOCK, d),
                     lambda b, i: (b, jnp.minimum((i + 1) * nb_tile, nblk_seq - 1), 0)),
        _const_spec(ling.shape), _const_spec(linb.shape), _const_spec(win.shape),
        _const_spec(gg.shape), _const_spec(gb.shape), _const_spec(wcat.shape),
        _const_spec(bsp.shape),
        _const_spec(cw.shape), _const_spec(cb.shape), _const_spec(clg.shape),
        _const_spec(clb.shape),
        pl.BlockSpec(memory_space=pltpu.SMEM),
        _const_spec(biasm.shape), _const_spec(wout.shape),
        _const_spec(pg.shape), _const_spec(pb.shape),
    ]
    scratch = [
        pltpu.VMEM((tm, d), _BF16),
        pltpu.VMEM((tm, d), _F32),
        pltpu.VMEM((tm, 3 * W_A), _F32),
        pltpu.VMEM((tm, 3 * W_B), _F32),
        pltpu.VMEM((tm, W_C), _BF16),
        pltpu.VMEM((tm, W_C), _F32),
        pltpu.VMEM((tm, 2 * KV_W), _F32),
        pltpu.VMEM((8, tm + 2 * BLOCK, LANES), _BF16),
        pltpu.VMEM((tm + 2 * CONV_HALO, W_B), _F32),
        pltpu.VMEM((tm, d), _BF16),
        pltpu.VMEM((tm, d), _F32),
    ]
    return pl.pallas_call(
        functools.partial(_layer_kernel, first_layer, tm, nblk_seq),
        grid=grid,
        in_specs=in_specs,
        out_specs=pl.BlockSpec((None, tm, d), lambda b, i: (b, i, 0)),
        out_shape=jax.ShapeDtypeStruct(x.shape, x.dtype),
        scratch_shapes=scratch,
        compiler_params=pltpu.CompilerParams(
            dimension_semantics=("arbitrary", "arbitrary"),
            vmem_limit_bytes=VMEM_LIMIT_BYTES),
        name="layer_first" if first_layer else "layer_next",
    )(x, x, x, ling, linb, win, gg, gb, wcat, bsp, cw, cb, clg, clb, sink, biasm, wout, pg, pb)


def kernel(x, ln_in_g, ln_in_b, w_in, gmlp_ln_g, gmlp_ln_b, w_spatial, b_spatial, conv_w, conv_b,
           conv_ln_g, conv_ln_b, attn_sink, rel_bias, w_out, post_ln_g, post_ln_b):
    depth = w_in.shape[0]
    assert depth == DEPTH and x.shape[2] == D_MODEL and w_in.shape[2] == D_IN

    qq = jnp.arange(BLOCK)[:, None]
    kk = jnp.arange(3 * BLOCK)[None, :]
    bucket = _t5_bucket(kk - BLOCK - qq).astype(jnp.int32)
    biasm = pl.pallas_call(
        _bias_kernel,
        in_specs=[pl.BlockSpec(memory_space=pltpu.VMEM), pl.BlockSpec(memory_space=pltpu.SMEM)],
        out_specs=pl.BlockSpec(memory_space=pltpu.VMEM),
        out_shape=jax.ShapeDtypeStruct((3, N_Q_HEADS, BLOCK, 3 * BLOCK), _F32),
        name="rel_bias_tables",
    )(bucket, rel_bias.astype(_F32))

    row = lambda a: a.reshape(1, -1).astype(_F32)
    for l in range(depth):
        wcat = jnp.transpose(w_spatial[l], (1, 0, 2)).reshape(CHUNK, N_HEADS_A * CHUNK).astype(_BF16)
        bsp = jnp.repeat(b_spatial[l].T, HEAD_DIM, axis=1).astype(_F32)
        x = _layer_call(
            l == 0, x, row(ln_in_g), row(ln_in_b), w_in[l].astype(_BF16),
            row(gmlp_ln_g[l]), row(gmlp_ln_b[l]), wcat, bsp,
            conv_w[l].astype(_F32), row(conv_b[l]), row(conv_ln_g[l]), row(conv_ln_b[l]),
            attn_sink[l].astype(_F32), biasm, w_out[l].astype(_BF16),
            row(post_ln_g[l]), row(post_ln_b[l]))
    return x
```

```python
import functools
import math

import jax
import jax.numpy as jnp
from jax import lax
from jax.experimental import pallas as pl
from jax.experimental.pallas import tpu as pltpu

D_MODEL = 1024
HEAD_DIM = 64
W_A = 256
W_B = 256
W_C = 512
N_HEADS_A = 4
N_Q_HEADS = 8
N_KV_HEADS = 2
KV_W = N_KV_HEADS * HEAD_DIM
CONV_WIDTH = 31
CONV_PAD = CONV_WIDTH // 2
BLOCK = 128
N_BUCKETS = 32
MAX_DISTANCE = 128
LN_EPS = 1e-5
NEG_INF = -1e30
DEPTH = 2
DEEPNORM_ALPHA = (2 * DEPTH) ** 0.25

COL_A = 0
COL_B = 3 * W_A
COL_Q = COL_B + 3 * W_B
COL_KV = COL_Q + W_C
COL_CG = COL_KV + 2 * KV_W
D_IN = COL_CG + W_C

LANES = 128
SUBLANES = 8
CONV_HALO = 16
CONV_ROWS = 64
H_SLOTS = 2
BAND_SLOTS = 4
VMEM_LIMIT_BYTES = 48 * 1024 * 1024

_BF16 = jnp.bfloat16
_F32 = jnp.float32


def _layer_norm(x, g, b):
    mu = jnp.mean(x, axis=-1, keepdims=True)
    xc = x - mu
    var = jnp.mean(xc * xc, axis=-1, keepdims=True)
    return xc * lax.rsqrt(var + LN_EPS) * g + b


def _dot(a, b):
    return jnp.dot(a, b, preferred_element_type=_F32)


def _bias_kernel(bucket_ref, rb_ref, out_ref):
    bucket = bucket_ref[...]
    row = lax.broadcasted_iota(jnp.int32, (BLOCK, 3 * BLOCK), 0)
    col = lax.broadcasted_iota(jnp.int32, (BLOCK, 3 * BLOCK), 1)
    in_window = jnp.abs(col - BLOCK - row) <= BLOCK
    for h in range(N_Q_HEADS):
        acc = jnp.zeros((BLOCK, 3 * BLOCK), _F32)
        for b in range(N_BUCKETS):
            acc = jnp.where(bucket == b, rb_ref[b, h], acc)
        base = jnp.where(in_window, acc, NEG_INF)
        out_ref[0, h] = base
        out_ref[1, h] = jnp.where(col < BLOCK, NEG_INF, base)
        out_ref[2, h] = jnp.where(col >= 2 * BLOCK, NEG_INF, base)


def _layer_kernel(first_layer, tm, nblk_seq,
                  xc_ref, xn_ref, ling_ref, linb_ref, win_ref,
                  gg_ref, gb_ref, wcat_ref, bsp_ref,
                  cw_ref, cb_ref, clg_ref, clb_ref,
                  sink_ref, biasm_ref, wout_ref, pg_ref, pb_ref,
                  out_ref,
                  xb_s, xres_r, ha_r, bg_r, q_r, cg_r, kvar_r, y_r):
    i = pl.program_id(1)
    nblk = tm // BLOCK
    low = lax.broadcasted_iota(jnp.int32, (BLOCK, LANES), 1) < HEAD_DIM
    head_of_lane = lax.broadcasted_iota(jnp.int32, (BLOCK, W_A), 1) // HEAD_DIM

    def project_pieces(x_blk, g_new):
        hs = g_new & (H_SLOTS - 1)
        bs = g_new & (BAND_SLOTS - 1)
        if first_layer:
            x_blk = _layer_norm(x_blk, ling_ref[...], linb_ref[...])
            xres_r[hs] = x_blk
        xb_s[...] = x_blk.astype(_BF16)

        def piece_a():
            ha_r[hs] = _dot(xb_s[...], win_ref[:, COL_A:COL_B])

        def piece_b():
            hb = _dot(xb_s[...], win_ref[:, COL_B:COL_Q])
            bg_r[hs] = hb[:, 2 * W_B:3 * W_B]
            y = hb[:, 0:W_B] * jax.nn.sigmoid(hb[:, W_B:2 * W_B])
            y_r[bs] = jnp.where(g_new < nblk_seq, y, 0.0)

        def piece_q():
            q_r[hs] = (_dot(xb_s[...], win_ref[:, COL_Q:COL_KV]) * (HEAD_DIM ** -0.5)).astype(_BF16)

        def piece_kv():
            kv = _dot(xb_s[...], win_ref[:, COL_KV:COL_CG])
            for t in range(2):
                nat = kv[:, t * KV_W:(t + 1) * KV_W]
                swp = pltpu.roll(nat, HEAD_DIM, axis=1)
                kvar_r[bs, 4 * t + 0] = jnp.where(low, nat, 0.0).astype(_BF16)
                kvar_r[bs, 4 * t + 1] = jnp.where(low, 0.0, swp).astype(_BF16)
                kvar_r[bs, 4 * t + 2] = jnp.where(low, swp, 0.0).astype(_BF16)
                kvar_r[bs, 4 * t + 3] = jnp.where(low, 0.0, nat).astype(_BF16)

        def piece_cg():
            cg_r[hs] = _dot(xb_s[...], win_ref[:, COL_CG:D_IN])

        return [piece_kv, piece_b, piece_a, piece_q, piece_cg]

    def mix(j, g, pieces):
        pieces = list(pieces)

        def issue(n=1):
            for _ in range(n):
                if pieces:
                    pieces.pop(0)()
        rows = pl.ds(pl.multiple_of(j * BLOCK, BLOCK), BLOCK)
        hs = g & (H_SLOTS - 1)
        s_prev = (g + BAND_SLOTS - 1) & (BAND_SLOTS - 1)
        s_cur = g & (BAND_SLOTS - 1)
        s_next = (g + 1) & (BAND_SLOTS - 1)

        issue(2)
        ha = ha_r[hs]
        u = jax.nn.gelu(ha[:, 0:W_A])
        v = _layer_norm(jax.nn.gelu(ha[:, W_A:2 * W_A]), gg_ref[...], gb_ref[...])
        gate_a = jax.nn.silu(ha[:, 2 * W_A:3 * W_A])
        vb = v.astype(_BF16)
        zero = jnp.zeros_like(vb)
        rhs = jnp.concatenate([jnp.where(head_of_lane == h, vb, zero)
                               for h in range(N_HEADS_A)], axis=0)
        sp = _dot(wcat_ref[...], rhs) + bsp_ref[...]
        ya = (u * sp * gate_a).astype(_BF16)

        ywin = jnp.concatenate([y_r[s_prev, BLOCK - CONV_HALO:BLOCK, :], y_r[s_cur],
                                y_r[s_next, 0:CONV_HALO, :]], axis=0)
        bg = bg_r[hs]
        span = CONV_ROWS + 2 * CONV_HALO
        yb_parts = []
        for ci in range(BLOCK // CONV_ROWS):
            issue()
            base = ci * CONV_ROWS
            ych = ywin[base:base + span, :]
            acc = jnp.zeros((CONV_ROWS, W_B), _F32)
            for r in range(SUBLANES):
                zr = ych if r == 0 else pltpu.roll(ych, span - r, axis=0)
                for m in range(2 * CONV_HALO // SUBLANES):
                    k = SUBLANES * m + r - 1
                    if 0 <= k < CONV_WIDTH:
                        acc = acc + zr[SUBLANES * m:SUBLANES * m + CONV_ROWS, :] * cw_ref[k:k + 1, :]
            z = _layer_norm(acc + cb_ref[...], clg_ref[...], clb_ref[...])
            gate_b = jax.nn.silu(bg[base:base + CONV_ROWS, :])
            yb_parts.append((jax.nn.silu(z) * gate_b).astype(_BF16))
        yb = jnp.concatenate(yb_parts, axis=0)

        edge = jnp.where(g == 0, 1, jnp.where(g == nblk_seq - 1, 2, 0))
        q_blk = q_r[hs]
        cg = cg_r[hs]

        def band(var):
            return [kvar_r[s_prev, var], kvar_r[s_cur, var], kvar_r[s_next, var]]

        yc_parts = []
        for kvh in range(N_KV_HEADS):
            issue()
            qg = q_blk[:, kvh * 2 * LANES:(kvh + 1) * 2 * LANES]
            lhs = jnp.concatenate([qg[:, 0:LANES], qg[:, LANES:2 * LANES]], axis=0)
            kcat = jnp.concatenate(band(2 * kvh) + band(2 * kvh + 1), axis=0)
            s2 = lax.dot_general(lhs, kcat, (((1,), (1,)), ((), ())),
                                 preferred_element_type=_F32)
            p_rows = []
            r_inv = []
            for pp in range(2):
                p_pair = []
                for hh in range(2):
                    h = 4 * kvh + 2 * pp + hh
                    s = (s2[pp * BLOCK:(pp + 1) * BLOCK, hh * 3 * BLOCK:(hh + 1) * 3 * BLOCK]
                         + biasm_ref[edge, h])
                    sk = sink_ref[h]
                    m = jnp.maximum(jnp.max(s, axis=-1, keepdims=True), sk)
                    p = jnp.exp(s - m)
                    den = jnp.sum(p, axis=-1, keepdims=True) + jnp.exp(sk - m)
                    r_inv.append(1.0 / den)
                    p_pair.append(p.astype(_BF16))
                p_rows.append(jnp.concatenate(p_pair, axis=1))
            pcat = jnp.concatenate(p_rows, axis=0)
            vcat = jnp.concatenate(band(4 + 2 * kvh) + band(5 + 2 * kvh), axis=0)
            o2 = _dot(pcat, vcat)
            for pp in range(2):
                cols = slice((2 * kvh + pp) * LANES, (2 * kvh + pp + 1) * LANES)
                scale = jnp.where(low, r_inv[2 * pp], r_inv[2 * pp + 1])
                o = o2[pp * BLOCK:(pp + 1) * BLOCK, :] * scale
                yc_parts.append((o * jax.nn.silu(cg[:, cols])).astype(_BF16))

        issue(len(pieces))
        ymix = jnp.concatenate([ya, yb] + yc_parts, axis=1)
        y_out = _dot(ymix, wout_ref[...])
        x_res = xres_r[hs] if first_layer else xc_ref[rows, :]
        out_ref[rows, :] = _layer_norm(DEEPNORM_ALPHA * x_res + y_out, pg_ref[...], pb_ref[...])

    g0 = i * nblk

    @pl.when(i == 0)
    def _():
        kvar_r[BAND_SLOTS - 1] = jnp.zeros(kvar_r.shape[1:], _BF16)
        y_r[BAND_SLOTS - 1] = jnp.zeros(y_r.shape[1:], _F32)
        for piece in project_pieces(xc_ref[0:BLOCK, :], g0):
            piece()

    def body(j, carry):
        g = g0 + j
        nxt = pl.multiple_of(jnp.minimum(j + 1, nblk - 1) * BLOCK, BLOCK)
        x_next = jnp.where(j + 1 < nblk, xc_ref[pl.ds(nxt, BLOCK), :], xn_ref[...])
        mix(j, g, project_pieces(x_next, g + 1))
        return carry

    lax.fori_loop(0, nblk, body, 0)


def _t5_bucket(rel):
    nb = N_BUCKETS // 2
    max_exact = nb // 2
    ret = jnp.where(rel > 0, nb, 0)
    n = jnp.abs(rel)
    nf = jnp.maximum(n, 1).astype(jnp.float32)
    large = max_exact + (jnp.log(nf / max_exact) / math.log(MAX_DISTANCE / max_exact)
                         * (nb - max_exact)).astype(jnp.int32)
    large = jnp.minimum(large, nb - 1)
    return ret + jnp.where(n < max_exact, n, large)


def _const_spec(shape):
    zeros = (0,) * len(shape)
    return pl.BlockSpec(shape, lambda b, i: zeros, pipeline_mode=pl.Buffered(1))


def _tile_rows(seq):
    tm = 512
    assert seq % tm == 0 and seq // BLOCK >= 2
    return tm


def _layer_call(first_layer, x, ling, linb, win, gg, gb, wcat, bsp, cw, cb, clg, clb,
                sink, biasm, wout, pg, pb):
    bsz, seq, d = x.shape
    tm = _tile_rows(seq)
    nb_tile = tm // BLOCK
    nblk_seq = seq // BLOCK
    grid = (bsz, seq // tm)

    in_specs = [
        pl.BlockSpec((None, tm, d), lambda b, i: (b, i, 0)),
        pl.BlockSpec((None, BLOCK, d),
                     lambda b, i: (b, jnp.minimum((i + 1) * nb_tile, nblk_seq - 1), 0)),
        _const_spec(ling.shape), _const_spec(linb.shape), _const_spec(win.shape),
        _const_spec(gg.shape), _const_spec(gb.shape), _const_spec(wcat.shape),
        _const_spec(bsp.shape),
        _const_spec(cw.shape), _const_spec(cb.shape), _const_spec(clg.shape),
        _const_spec(clb.shape),
        pl.BlockSpec(memory_space=pltpu.SMEM),
        _const_spec(biasm.shape), _const_spec(wout.shape),
        _const_spec(pg.shape), _const_spec(pb.shape),
    ]
    scratch = [
        pltpu.VMEM((BLOCK, d), _BF16),
        pltpu.VMEM((H_SLOTS, BLOCK, d), _F32),
        pltpu.VMEM((H_SLOTS, BLOCK, 3 * W_A), _F32),
        pltpu.VMEM((H_SLOTS, BLOCK, W_B), _F32),
        pltpu.VMEM((H_SLOTS, BLOCK, W_C), _BF16),
        pltpu.VMEM((H_SLOTS, BLOCK, W_C), _F32),
        pltpu.VMEM((BAND_SLOTS, 8, BLOCK, LANES), _BF16),
        pltpu.VMEM((BAND_SLOTS, BLOCK, W_B), _F32),
    ]
    return pl.pallas_call(
        functools.partial(_layer_kernel, first_layer, tm, nblk_seq),
        grid=grid,
        in_specs=in_specs,
        out_specs=pl.BlockSpec((None, tm, d), lambda b, i: (b, i, 0)),
        out_shape=jax.ShapeDtypeStruct(x.shape, x.dtype),
        scratch_shapes=scratch,
        compiler_params=pltpu.CompilerParams(
            dimension_semantics=("arbitrary", "arbitrary"),
            vmem_limit_bytes=VMEM_LIMIT_BYTES),
        name="layer_first" if first_layer else "layer_next",
    )(x, x, ling, linb, win, gg, gb, wcat, bsp, cw, cb, clg, clb, sink, biasm, wout, pg, pb)


def kernel(x, ln_in_g, ln_in_b, w_in, gmlp_ln_g, gmlp_ln_b, w_spatial, b_spatial, conv_w, conv_b,
           conv_ln_g, conv_ln_b, attn_sink, rel_bias, w_out, post_ln_g, post_ln_b):
    depth = w_in.shape[0]
    assert depth == DEPTH and x.shape[2] == D_MODEL and w_in.shape[2] == D_IN

    qq = jnp.arange(BLOCK)[:, None]
    kk = jnp.arange(3 * BLOCK)[None, :]
    bucket = _t5_bucket(kk - BLOCK - qq).astype(jnp.int32)
    biasm = pl.pallas_call(
        _bias_kernel,
        in_specs=[pl.BlockSpec(memory_space=pltpu.VMEM), pl.BlockSpec(memory_space=pltpu.SMEM)],
        out_specs=pl.BlockSpec(memory_space=pltpu.VMEM),
        out_shape=jax.ShapeDtypeStruct((3, N_Q_HEADS, BLOCK, 3 * BLOCK), _F32),
        name="rel_bias_tables",
    )(bucket, rel_bias.astype(_F32))

    row = lambda a: a.reshape(1, -1).astype(_F32)
    for l in range(depth):
        wcat = jnp.transpose(w_spatial[l], (1, 0, 2)).reshape(BLOCK, N_HEADS_A * BLOCK).astype(_BF16)
        bsp = jnp.repeat(b_spatial[l].T, HEAD_DIM, axis=1).astype(_F32)
        x = _layer_call(
            l == 0, x, row(ln_in_g), row(ln_in_b), w_in[l].astype(_BF16),
            row(gmlp_ln_g[l]), row(gmlp_ln_b[l]), wcat, bsp,
            conv_w[l].astype(_F32), row(conv_b[l]), row(conv_ln_g[l]), row(conv_ln_b[l]),
            attn_sink[l].astype(_F32), biasm, w_out[l].astype(_BF16),
            row(post_ln_g[l]), row(post_ln_b[l]))
    return x
```

```python
import functools
import math

import jax
import jax.numpy as jnp
from jax import lax
from jax.experimental import pallas as pl
from jax.experimental.pallas import tpu as pltpu

D_MODEL = 1024
HEAD_DIM = 64
W_A = 256
W_B = 256
W_C = 512
N_HEADS_A = 4
N_Q_HEADS = 8
N_KV_HEADS = 2
KV_W = N_KV_HEADS * HEAD_DIM
CONV_WIDTH = 31
CONV_PAD = CONV_WIDTH // 2
BLOCK = 128
N_BUCKETS = 32
MAX_DISTANCE = 128
LN_EPS = 1e-5
NEG_INF = -1e30
DEPTH = 2
DEEPNORM_ALPHA = (2 * DEPTH) ** 0.25

COL_A = 0
COL_B = 3 * W_A
COL_Q = COL_B + 3 * W_B
COL_KV = COL_Q + W_C
COL_CG = COL_KV + 2 * KV_W
D_IN = COL_CG + W_C

LANES = 128
SUBLANES = 8
CONV_HALO = 16
CONV_ROWS = 64
H_SLOTS = 2
BAND_SLOTS = 4
TILE_ROWS = 2048
VMEM_LIMIT_BYTES = 56 * 1024 * 1024

_BF16 = jnp.bfloat16
_F32 = jnp.float32


def _layer_norm(x, g, b):
    mu = jnp.mean(x, axis=-1, keepdims=True)
    xc = x - mu
    var = jnp.mean(xc * xc, axis=-1, keepdims=True)
    return xc * lax.rsqrt(var + LN_EPS) * g + b


def _dot(a, b):
    return jnp.dot(a, b, preferred_element_type=_F32)


def _bias_kernel(bucket_ref, rb_ref, out_ref):
    bucket = bucket_ref[...]
    row = lax.broadcasted_iota(jnp.int32, (BLOCK, 3 * BLOCK), 0)
    col = lax.broadcasted_iota(jnp.int32, (BLOCK, 3 * BLOCK), 1)
    in_window = jnp.abs(col - BLOCK - row) <= BLOCK
    for h in range(N_Q_HEADS):
        acc = jnp.zeros((BLOCK, 3 * BLOCK), _F32)
        for b in range(N_BUCKETS):
            acc = jnp.where(bucket == b, rb_ref[b, h], acc)
        base = jnp.where(in_window, acc, NEG_INF)
        out_ref[0, h] = base
        out_ref[1, h] = jnp.where(col < BLOCK, NEG_INF, base)
        out_ref[2, h] = jnp.where(col >= 2 * BLOCK, NEG_INF, base)


def _layer_kernel(first_layer, tm, nblk_seq,
                  xc_ref, xn_ref, ling_ref, linb_ref, win_ref,
                  gg_ref, gb_ref, wcat_ref, bsp_ref,
                  cw_ref, cb_ref, clg_ref, clb_ref,
                  sink_ref, biasm_ref, wout_ref, pg_ref, pb_ref,
                  out_ref,
                  xb_s, xres_r, ha_r, bg_r, q_r, cg_r, kvar_r, y_r):
    i = pl.program_id(1)
    nblk = tm // BLOCK
    low = lax.broadcasted_iota(jnp.int32, (BLOCK, LANES), 1) < HEAD_DIM
    head_of_lane = lax.broadcasted_iota(jnp.int32, (BLOCK, W_A), 1) // HEAD_DIM

    def project_pieces(x_blk, g_new):
        hs = g_new & (H_SLOTS - 1)
        bs = g_new & (BAND_SLOTS - 1)
        if first_layer:
            x_blk = _layer_norm(x_blk, ling_ref[...], linb_ref[...])
            xres_r[hs] = x_blk
        xb_s[...] = x_blk.astype(_BF16)

        def piece_a():
            ha_r[hs] = _dot(xb_s[...], win_ref[:, COL_A:COL_B])

        def piece_b():
            hb = _dot(xb_s[...], win_ref[:, COL_B:COL_Q])
            bg_r[hs] = hb[:, 2 * W_B:3 * W_B]
            y = hb[:, 0:W_B] * jax.nn.sigmoid(hb[:, W_B:2 * W_B])
            y_r[bs] = jnp.where(g_new < nblk_seq, y, 0.0)

        def piece_q():
            q_r[hs] = (_dot(xb_s[...], win_ref[:, COL_Q:COL_KV]) * (HEAD_DIM ** -0.5)).astype(_BF16)

        def piece_kv():
            kv = _dot(xb_s[...], win_ref[:, COL_KV:COL_CG])
            for t in range(2):
                nat = kv[:, t * KV_W:(t + 1) * KV_W]
                swp = pltpu.roll(nat, HEAD_DIM, axis=1)
                kvar_r[bs, 4 * t + 0] = jnp.where(low, nat, 0.0).astype(_BF16)
                kvar_r[bs, 4 * t + 1] = jnp.where(low, 0.0, swp).astype(_BF16)
                kvar_r[bs, 4 * t + 2] = jnp.where(low, swp, 0.0).astype(_BF16)
                kvar_r[bs, 4 * t + 3] = jnp.where(low, 0.0, nat).astype(_BF16)

        def piece_cg():
            cg_r[hs] = _dot(xb_s[...], win_ref[:, COL_CG:D_IN])

        return [piece_kv, piece_b, piece_a, piece_q, piece_cg]

    def mix(j, g, pieces):
        pieces = list(pieces)

        def issue(n=1):
            for _ in range(n):
                if pieces:
                    pieces.pop(0)()
        rows = pl.ds(pl.multiple_of(j * BLOCK, BLOCK), BLOCK)
        hs = g & (H_SLOTS - 1)
        s_prev = (g + BAND_SLOTS - 1) & (BAND_SLOTS - 1)
        s_cur = g & (BAND_SLOTS - 1)
        s_next = (g + 1) & (BAND_SLOTS - 1)

        issue(2)
        ha = ha_r[hs]
        u = jax.nn.gelu(ha[:, 0:W_A])
        v = _layer_norm(jax.nn.gelu(ha[:, W_A:2 * W_A]), gg_ref[...], gb_ref[...])
        gate_a = jax.nn.silu(ha[:, 2 * W_A:3 * W_A])
        vb = v.astype(_BF16)
        zero = jnp.zeros_like(vb)
        rhs = jnp.concatenate([jnp.where(head_of_lane == h, vb, zero)
                               for h in range(N_HEADS_A)], axis=0)
        sp = _dot(wcat_ref[...], rhs) + bsp_ref[...]
        ya = (u * sp * gate_a).astype(_BF16)

        ywin = jnp.concatenate([y_r[s_prev, BLOCK - CONV_HALO:BLOCK, :], y_r[s_cur],
                                y_r[s_next, 0:CONV_HALO, :]], axis=0)
        bg = bg_r[hs]
        span = CONV_ROWS + 2 * CONV_HALO
        yb_parts = []
        for ci in range(BLOCK // CONV_ROWS):
            issue()
            base = ci * CONV_ROWS
            ych = ywin[base:base + span, :]
            acc = jnp.zeros((CONV_ROWS, W_B), _F32)
            for r in range(SUBLANES):
                zr = ych if r == 0 else pltpu.roll(ych, span - r, axis=0)
                for m in range(2 * CONV_HALO // SUBLANES):
                    k = SUBLANES * m + r - 1
                    if 0 <= k < CONV_WIDTH:
                        acc = acc + zr[SUBLANES * m:SUBLANES * m + CONV_ROWS, :] * cw_ref[k:k + 1, :]
            z = _layer_norm(acc + cb_ref[...], clg_ref[...], clb_ref[...])
            gate_b = jax.nn.silu(bg[base:base + CONV_ROWS, :])
            yb_parts.append((jax.nn.silu(z) * gate_b).astype(_BF16))
        yb = jnp.concatenate(yb_parts, axis=0)

        edge = jnp.where(g == 0, 1, jnp.where(g == nblk_seq - 1, 2, 0))
        q_blk = q_r[hs]
        cg = cg_r[hs]

        def band(var):
            return [kvar_r[s_prev, var], kvar_r[s_cur, var], kvar_r[s_next, var]]

        yc_parts = []
        for kvh in range(N_KV_HEADS):
            issue()
            qg = q_blk[:, kvh * 2 * LANES:(kvh + 1) * 2 * LANES]
            lhs = jnp.concatenate([qg[:, 0:LANES], qg[:, LANES:2 * LANES]], axis=0)
            kcat = jnp.concatenate(band(2 * kvh) + band(2 * kvh + 1), axis=0)
            s2 = lax.dot_general(lhs, kcat, (((1,), (1,)), ((), ())),
                                 preferred_element_type=_F32)
            p_rows = []
            r_inv = []
            for pp in range(2):
                p_pair = []
                for hh in range(2):
                    h = 4 * kvh + 2 * pp + hh
                    s = (s2[pp * BLOCK:(pp + 1) * BLOCK, hh * 3 * BLOCK:(hh + 1) * 3 * BLOCK]
                         + biasm_ref[edge, h])
                    sk = sink_ref[h]
                    m = jnp.maximum(jnp.max(s, axis=-1, keepdims=True), sk)
                    p = jnp.exp(s - m)
                    den = jnp.sum(p, axis=-1, keepdims=True) + jnp.exp(sk - m)
                    r_inv.append(1.0 / den)
                    p_pair.append(p.astype(_BF16))
                p_rows.append(jnp.concatenate(p_pair, axis=1))
            pcat = jnp.concatenate(p_rows, axis=0)
            vcat = jnp.concatenate(band(4 + 2 * kvh) + band(5 + 2 * kvh), axis=0)
            o2 = _dot(pcat, vcat)
            for pp in range(2):
                cols = slice((2 * kvh + pp) * LANES, (2 * kvh + pp + 1) * LANES)
                scale = jnp.where(low, r_inv[2 * pp], r_inv[2 * pp + 1])
                o = o2[pp * BLOCK:(pp + 1) * BLOCK, :] * scale
                yc_parts.append((o * jax.nn.silu(cg[:, cols])).astype(_BF16))

        issue(len(pieces))
        ymix = jnp.concatenate([ya, yb] + yc_parts, axis=1)
        y_out = _dot(ymix, wout_ref[...])
        x_res = xres_r[hs] if first_layer else xc_ref[rows, :]
        out_ref[rows, :] = _layer_norm(DEEPNORM_ALPHA * x_res + y_out, pg_ref[...], pb_ref[...])

    g0 = i * nblk

    @pl.when(i == 0)
    def _():
        kvar_r[BAND_SLOTS - 1] = jnp.zeros(kvar_r.shape[1:], _BF16)
        y_r[BAND_SLOTS - 1] = jnp.zeros(y_r.shape[1:], _F32)
        for piece in project_pieces(xc_ref[0:BLOCK, :], g0):
            piece()

    def body(j, carry):
        g = g0 + j
        nxt = pl.multiple_of(jnp.minimum(j + 1, nblk - 1) * BLOCK, BLOCK)
        x_next = jnp.where(j + 1 < nblk, xc_ref[pl.ds(nxt, BLOCK), :], xn_ref[...])
        mix(j, g, project_pieces(x_next, g + 1))
        return carry

    lax.fori_loop(0, nblk, body, 0)


def _t5_bucket(rel):
    nb = N_BUCKETS // 2
    max_exact = nb // 2
    ret = jnp.where(rel > 0, nb, 0)
    n = jnp.abs(rel)
    nf = jnp.maximum(n, 1).astype(jnp.float32)
    large = max_exact + (jnp.log(nf / max_exact) / math.log(MAX_DISTANCE / max_exact)
                         * (nb - max_exact)).astype(jnp.int32)
    large = jnp.minimum(large, nb - 1)
    return ret + jnp.where(n < max_exact, n, large)


def _const_spec(shape):
    zeros = (0,) * len(shape)
    return pl.BlockSpec(shape, lambda b, i: zeros, pipeline_mode=pl.Buffered(1))


def _layer_call(first_layer, x, ling, linb, win, gg, gb, wcat, bsp, cw, cb, clg, clb,
                sink, biasm, wout, pg, pb):
    bsz, seq, d = x.shape
    tm = TILE_ROWS
    assert seq % tm == 0 and seq // BLOCK >= 2
    nb_tile = tm // BLOCK
    nblk_seq = seq // BLOCK
    grid = (bsz, seq // tm)

    in_specs = [
        pl.BlockSpec((None, tm, d), lambda b, i: (b, i, 0)),
        pl.BlockSpec((None, BLOCK, d),
                     lambda b, i: (b, jnp.minimum((i + 1) * nb_tile, nblk_seq - 1), 0)),
        _const_spec(ling.shape), _const_spec(linb.shape), _const_spec(win.shape),
        _const_spec(gg.shape), _const_spec(gb.shape), _const_spec(wcat.shape),
        _const_spec(bsp.shape),
        _const_spec(cw.shape), _const_spec(cb.shape), _const_spec(clg.shape),
        _const_spec(clb.shape),
        pl.BlockSpec(memory_space=pltpu.SMEM),
        _const_spec(biasm.shape), _const_spec(wout.shape),
        _const_spec(pg.shape), _const_spec(pb.shape),
    ]
    scratch = [
        pltpu.VMEM((BLOCK, d), _BF16),
        pltpu.VMEM((H_SLOTS, BLOCK, d), _F32),
        pltpu.VMEM((H_SLOTS, BLOCK, 3 * W_A), _F32),
        pltpu.VMEM((H_SLOTS, BLOCK, W_B), _F32),
        pltpu.VMEM((H_SLOTS, BLOCK, W_C), _BF16),
        pltpu.VMEM((H_SLOTS, BLOCK, W_C), _F32),
        pltpu.VMEM((BAND_SLOTS, 8, BLOCK, LANES), _BF16),
        pltpu.VMEM((BAND_SLOTS, BLOCK, W_B), _F32),
    ]
    return pl.pallas_call(
        functools.partial(_layer_kernel, first_layer, tm, nblk_seq),
        grid=grid,
        in_specs=in_specs,
        out_specs=pl.BlockSpec((None, tm, d), lambda b, i: (b, i, 0)),
        out_shape=jax.ShapeDtypeStruct(x.shape, x.dtype),
        scratch_shapes=scratch,
        compiler_params=pltpu.CompilerParams(
            dimension_semantics=("arbitrary", "arbitrary"),
            vmem_limit_bytes=VMEM_LIMIT_BYTES),
        name="layer_first" if first_layer else "layer_next",
    )(x, x, ling, linb, win, gg, gb, wcat, bsp, cw, cb, clg, clb, sink, biasm, wout, pg, pb)


def kernel(x, ln_in_g, ln_in_b, w_in, gmlp_ln_g, gmlp_ln_b, w_spatial, b_spatial, conv_w, conv_b,
           conv_ln_g, conv_ln_b, attn_sink, rel_bias, w_out, post_ln_g, post_ln_b):
    depth = w_in.shape[0]
    assert depth == DEPTH and x.shape[2] == D_MODEL and w_in.shape[2] == D_IN

    qq = jnp.arange(BLOCK)[:, None]
    kk = jnp.arange(3 * BLOCK)[None, :]
    bucket = _t5_bucket(kk - BLOCK - qq).astype(jnp.int32)
    biasm = pl.pallas_call(
        _bias_kernel,
        in_specs=[pl.BlockSpec(memory_space=pltpu.VMEM), pl.BlockSpec(memory_space=pltpu.SMEM)],
        out_specs=pl.BlockSpec(memory_space=pltpu.VMEM),
        out_shape=jax.ShapeDtypeStruct((3, N_Q_HEADS, BLOCK, 3 * BLOCK), _F32),
        name="rel_bias_tables",
    )(bucket, rel_bias.astype(_F32))

    row = lambda a: a.reshape(1, -1).astype(_F32)
    for l in range(depth):
        wcat = jnp.transpose(w_spatial[l], (1, 0, 2)).reshape(BLOCK, N_HEADS_A * BLOCK).astype(_BF16)
        bsp = jnp.repeat(b_spatial[l].T, HEAD_DIM, axis=1).astype(_F32)
        x = _layer_call(
            l == 0, x, row(ln_in_g), row(ln_in_b), w_in[l].astype(_BF16),
            row(gmlp_ln_g[l]), row(gmlp_ln_b[l]), wcat, bsp,
            conv_w[l].astype(_F32), row(conv_b[l]), row(conv_ln_g[l]), row(conv_ln_b[l]),
            attn_sink[l].astype(_F32), biasm, w_out[l].astype(_BF16),
            row(post_ln_g[l]), row(post_ln_b[l]))
    return x
```

```python
import functools
import math

import jax
import jax.numpy as jnp
from jax import lax
from jax.experimental import pallas as pl
from jax.experimental.pallas import tpu as pltpu

D_MODEL = 1024
HEAD_DIM = 64
W_A = 256
W_B = 256
W_C = 512
N_HEADS_A = 4
N_Q_HEADS = 8
N_KV_HEADS = 2
KV_W = N_KV_HEADS * HEAD_DIM
CONV_WIDTH = 31
CONV_PAD = CONV_WIDTH // 2
BLOCK = 128
N_BUCKETS = 32
MAX_DISTANCE = 128
LN_EPS = 1e-5
NEG_INF = -1e30
DEPTH = 2
DEEPNORM_ALPHA = (2 * DEPTH) ** 0.25

COL_A = 0
COL_B = 3 * W_A
COL_Q = COL_B + 3 * W_B
COL_KV = COL_Q + W_C
COL_CG = COL_KV + 2 * KV_W
D_IN = COL_CG + W_C

LANES = 128
SUBLANES = 8
CONV_HALO = 16
CONV_ROWS = 64
PAIR = 2 * BLOCK
H_SLOTS = 2
BAND_SLOTS = 4
TILE_ROWS = 1024
VMEM_LIMIT_BYTES = 56 * 1024 * 1024

_BF16 = jnp.bfloat16
_F32 = jnp.float32


def _layer_norm(x, g, b):
    mu = jnp.mean(x, axis=-1, keepdims=True)
    xc = x - mu
    var = jnp.mean(xc * xc, axis=-1, keepdims=True)
    return xc * lax.rsqrt(var + LN_EPS) * g + b


def _dot(a, b):
    return jnp.dot(a, b, preferred_element_type=_F32)


def _bias_kernel(bucket_ref, rb_ref, out_ref):
    bucket = bucket_ref[...]
    row = lax.broadcasted_iota(jnp.int32, (BLOCK, 3 * BLOCK), 0)
    col = lax.broadcasted_iota(jnp.int32, (BLOCK, 3 * BLOCK), 1)
    in_window = jnp.abs(col - BLOCK - row) <= BLOCK
    for h in range(N_Q_HEADS):
        acc = jnp.zeros((BLOCK, 3 * BLOCK), _F32)
        for b in range(N_BUCKETS):
            acc = jnp.where(bucket == b, rb_ref[b, h], acc)
        base = jnp.where(in_window, acc, NEG_INF)
        out_ref[0, h] = base
        out_ref[1, h] = jnp.where(col < BLOCK, NEG_INF, base)
        out_ref[2, h] = jnp.where(col >= 2 * BLOCK, NEG_INF, base)


def _layer_kernel(first_layer, tm, nblk_seq,
                  xc_ref, xn_ref, ling_ref, linb_ref, win_ref,
                  gg_ref, gb_ref, wcat_ref, bsp_ref,
                  cw_ref, cb_ref, clg_ref, clb_ref,
                  sink_ref, biasm_ref, wout_ref, pg_ref, pb_ref,
                  out_ref,
                  xb_s, xres_r, ha_r, bg_r, q_r, cg_r, kvar_r, y_r):
    i = pl.program_id(1)
    npair = tm // PAIR
    npair_seq = nblk_seq // 2
    low = lax.broadcasted_iota(jnp.int32, (1, LANES), 1) < HEAD_DIM
    head_of_lane = lax.broadcasted_iota(jnp.int32, (1, W_A), 1) // HEAD_DIM

    def project_pieces(x_pair, p_new):
        hs = p_new & (H_SLOTS - 1)
        bs = p_new & (BAND_SLOTS - 1)
        if first_layer:
            x_pair = _layer_norm(x_pair, ling_ref[...], linb_ref[...])
            xres_r[hs] = x_pair
        xb_s[...] = x_pair.astype(_BF16)

        def piece_a():
            ha_r[hs] = _dot(xb_s[...], win_ref[:, COL_A:COL_B])

        def piece_b():
            hb = _dot(xb_s[...], win_ref[:, COL_B:COL_Q])
            bg_r[hs] = hb[:, 2 * W_B:3 * W_B]
            y = hb[:, 0:W_B] * jax.nn.sigmoid(hb[:, W_B:2 * W_B])
            y_r[bs] = jnp.where(p_new < npair_seq, y, 0.0)

        def piece_q():
            q_r[hs] = (_dot(xb_s[...], win_ref[:, COL_Q:COL_KV]) * (HEAD_DIM ** -0.5)).astype(_BF16)

        def piece_kv():
            kv = _dot(xb_s[...], win_ref[:, COL_KV:COL_CG])
            for t in range(2):
                nat = kv[:, t * KV_W:(t + 1) * KV_W]
                swp = pltpu.roll(nat, HEAD_DIM, axis=1)
                kvar_r[bs, 4 * t + 0] = jnp.where(low, nat, 0.0).astype(_BF16)
                kvar_r[bs, 4 * t + 1] = jnp.where(low, 0.0, swp).astype(_BF16)
                kvar_r[bs, 4 * t + 2] = jnp.where(low, swp, 0.0).astype(_BF16)
                kvar_r[bs, 4 * t + 3] = jnp.where(low, 0.0, nat).astype(_BF16)

        def piece_cg():
            cg_r[hs] = _dot(xb_s[...], win_ref[:, COL_CG:D_IN])

        return [piece_kv, piece_b, piece_a, piece_q, piece_cg]

    def mix_block(b, p, issue):
        hs = p & (H_SLOTS - 1)
        s_here = p & (BAND_SLOTS - 1)
        lo, hi = slice(0, BLOCK), slice(BLOCK, PAIR)
        here = lo if b == 0 else hi
        if b == 0:
            band_at = [((p + BAND_SLOTS - 1) & (BAND_SLOTS - 1), hi), (s_here, lo), (s_here, hi)]
        else:
            band_at = [(s_here, lo), (s_here, hi), ((p + 1) & (BAND_SLOTS - 1), lo)]
        g = 2 * p + b

        ha = ha_r[hs, here, :]
        u = jax.nn.gelu(ha[:, 0:W_A])
        v = _layer_norm(jax.nn.gelu(ha[:, W_A:2 * W_A]), gg_ref[...], gb_ref[...])
        gate_a = jax.nn.silu(ha[:, 2 * W_A:3 * W_A])
        vb = v.astype(_BF16)
        zero = jnp.zeros_like(vb)
        rhs = jnp.concatenate([jnp.where(head_of_lane == h, vb, zero)
                               for h in range(N_HEADS_A)], axis=0)
        sp = _dot(wcat_ref[...], rhs) + bsp_ref[...]
        ya = (u * sp * gate_a).astype(_BF16)

        (sp_, rp_), (sc_, rc_), (sn_, rn_) = band_at
        ywin = jnp.concatenate([y_r[sp_, rp_.stop - CONV_HALO:rp_.stop, :], y_r[sc_, rc_, :],
                                y_r[sn_, rn_.start:rn_.start + CONV_HALO, :]], axis=0)
        bg = bg_r[hs, here, :]
        span = CONV_ROWS + 2 * CONV_HALO
        yb_parts = []
        for ci in range(BLOCK // CONV_ROWS):
            issue()
            base = ci * CONV_ROWS
            ych = ywin[base:base + span, :]
            acc = jnp.zeros((CONV_ROWS, W_B), _F32)
            for r in range(SUBLANES):
                zr = ych if r == 0 else pltpu.roll(ych, span - r, axis=0)
                for m in range(2 * CONV_HALO // SUBLANES):
                    k = SUBLANES * m + r - 1
                    if 0 <= k < CONV_WIDTH:
                        acc = acc + zr[SUBLANES * m:SUBLANES * m + CONV_ROWS, :] * cw_ref[k:k + 1, :]
            z = _layer_norm(acc + cb_ref[...], clg_ref[...], clb_ref[...])
            gate_b = jax.nn.silu(bg[base:base + CONV_ROWS, :])
            yb_parts.append((jax.nn.silu(z) * gate_b).astype(_BF16))
        yb = jnp.concatenate(yb_parts, axis=0)

        edge = jnp.where(g == 0, 1, jnp.where(g == nblk_seq - 1, 2, 0))
        q_blk = q_r[hs, here, :]
        cg = cg_r[hs, here, :]

        def band(var):
            return [kvar_r[s_, var, r_, :] for (s_, r_) in band_at]

        yc_parts = []
        for kvh in range(N_KV_HEADS):
            issue()
            qg = q_blk[:, kvh * 2 * LANES:(kvh + 1) * 2 * LANES]
            lhs = jnp.concatenate([qg[:, 0:LANES], qg[:, LANES:2 * LANES]], axis=0)
            kcat = jnp.concatenate(band(2 * kvh) + band(2 * kvh + 1), axis=0)
            s2 = lax.dot_general(lhs, kcat, (((1,), (1,)), ((), ())),
                                 preferred_element_type=_F32)
            p_rows = []
            r_inv = []
            for pp in range(2):
                p_pair = []
                for hh in range(2):
                    h = 4 * kvh + 2 * pp + hh
                    s = (s2[pp * BLOCK:(pp + 1) * BLOCK, hh * 3 * BLOCK:(hh + 1) * 3 * BLOCK]
                         + biasm_ref[edge, h])
                    sk = sink_ref[h]
                    m = jnp.maximum(jnp.max(s, axis=-1, keepdims=True), sk)
                    p_ = jnp.exp(s - m)
                    den = jnp.sum(p_, axis=-1, keepdims=True) + jnp.exp(sk - m)
                    r_inv.append(1.0 / den)
                    p_pair.append(p_.astype(_BF16))
                p_rows.append(jnp.concatenate(p_pair, axis=1))
            pcat = jnp.concatenate(p_rows, axis=0)
            vcat = jnp.concatenate(band(4 + 2 * kvh) + band(5 + 2 * kvh), axis=0)
            o2 = _dot(pcat, vcat)
            for pp in range(2):
                cols = slice((2 * kvh + pp) * LANES, (2 * kvh + pp + 1) * LANES)
                scale = jnp.where(low, r_inv[2 * pp], r_inv[2 * pp + 1])
                o = o2[pp * BLOCK:(pp + 1) * BLOCK, :] * scale
                yc_parts.append((o * jax.nn.silu(cg[:, cols])).astype(_BF16))
        return jnp.concatenate([ya, yb] + yc_parts, axis=1)

    def mix(jp, p, pieces):
        pieces = list(pieces)

        def issue(n=1):
            for _ in range(n):
                if pieces:
                    pieces.pop(0)()
        rows = pl.ds(pl.multiple_of(jp * PAIR, PAIR), PAIR)
        hs = p & (H_SLOTS - 1)
        issue(2)
        ymix = jnp.concatenate([mix_block(0, p, issue), mix_block(1, p, issue)], axis=0)

        issue(len(pieces))
        y_out = _dot(ymix, wout_ref[...])
        x_res = xres_r[hs] if first_layer else xc_ref[rows, :]
        out_ref[rows, :] = _layer_norm(DEEPNORM_ALPHA * x_res + y_out, pg_ref[...], pb_ref[...])

    p0 = i * npair

    @pl.when(i == 0)
    def _():
        kvar_r[BAND_SLOTS - 1] = jnp.zeros(kvar_r.shape[1:], _BF16)
        y_r[BAND_SLOTS - 1] = jnp.zeros(y_r.shape[1:], _F32)
        for piece in project_pieces(xc_ref[0:PAIR, :], p0):
            piece()

    def body(jp, carry):
        p = p0 + jp
        nxt = pl.multiple_of(jnp.minimum(jp + 1, npair - 1) * PAIR, PAIR)
        x_next = jnp.where(jp + 1 < npair, xc_ref[pl.ds(nxt, PAIR), :], xn_ref[...])
        mix(jp, p, project_pieces(x_next, p + 1))
        return carry

    lax.fori_loop(0, npair, body, 0)


def _t5_bucket(rel):
    nb = N_BUCKETS // 2
    max_exact = nb // 2
    ret = jnp.where(rel > 0, nb, 0)
    n = jnp.abs(rel)
    nf = jnp.maximum(n, 1).astype(jnp.float32)
    large = max_exact + (jnp.log(nf / max_exact) / math.log(MAX_DISTANCE / max_exact)
                         * (nb - max_exact)).astype(jnp.int32)
    large = jnp.minimum(large, nb - 1)
    return ret + jnp.where(n < max_exact, n, large)


def _const_spec(shape):
    zeros = (0,) * len(shape)
    return pl.BlockSpec(shape, lambda b, i: zeros, pipeline_mode=pl.Buffered(1))


def _layer_call(first_layer, x, ling, linb, win, gg, gb, wcat, bsp, cw, cb, clg, clb,
                sink, biasm, wout, pg, pb):
    bsz, seq, d = x.shape
    tm = TILE_ROWS
    assert seq % tm == 0 and seq // BLOCK >= 2
    nb_tile = tm // BLOCK
    nblk_seq = seq // BLOCK
    grid = (bsz, seq // tm)

    in_specs = [
        pl.BlockSpec((None, tm, d), lambda b, i: (b, i, 0)),
        pl.BlockSpec((None, PAIR, d),
                     lambda b, i: (b, jnp.minimum((i + 1) * nb_tile // 2, nblk_seq // 2 - 1), 0)),
        _const_spec(ling.shape), _const_spec(linb.shape), _const_spec(win.shape),
        _const_spec(gg.shape), _const_spec(gb.shape), _const_spec(wcat.shape),
        _const_spec(bsp.shape),
        _const_spec(cw.shape), _const_spec(cb.shape), _const_spec(clg.shape),
        _const_spec(clb.shape),
        pl.BlockSpec(memory_space=pltpu.SMEM),
        _const_spec(biasm.shape), _const_spec(wout.shape),
        _const_spec(pg.shape), _const_spec(pb.shape),
    ]
    scratch = [
        pltpu.VMEM((PAIR, d), _BF16),
        pltpu.VMEM((H_SLOTS, PAIR, d), _F32),
        pltpu.VMEM((H_SLOTS, PAIR, 3 * W_A), _F32),
        pltpu.VMEM((H_SLOTS, PAIR, W_B), _F32),
        pltpu.VMEM((H_SLOTS, PAIR, W_C), _BF16),
        pltpu.VMEM((H_SLOTS, PAIR, W_C), _F32),
        pltpu.VMEM((BAND_SLOTS, 8, PAIR, LANES), _BF16),
        pltpu.VMEM((BAND_SLOTS, PAIR, W_B), _F32),
    ]
    return pl.pallas_call(
        functools.partial(_layer_kernel, first_layer, tm, nblk_seq),
        grid=grid,
        in_specs=in_specs,
        out_specs=pl.BlockSpec((None, tm, d), lambda b, i: (b, i, 0)),
        out_shape=jax.ShapeDtypeStruct(x.shape, x.dtype),
        scratch_shapes=scratch,
        compiler_params=pltpu.CompilerParams(
            dimension_semantics=("arbitrary", "arbitrary"),
            vmem_limit_bytes=VMEM_LIMIT_BYTES),
        name="layer_first" if first_layer else "layer_next",
    )(x, x, ling, linb, win, gg, gb, wcat, bsp, cw, cb, clg, clb, sink, biasm, wout, pg, pb)


def kernel(x, ln_in_g, ln_in_b, w_in, gmlp_ln_g, gmlp_ln_b, w_spatial, b_spatial, conv_w, conv_b,
           conv_ln_g, conv_ln_b, attn_sink, rel_bias, w_out, post_ln_g, post_ln_b):
    depth = w_in.shape[0]
    assert depth == DEPTH and x.shape[2] == D_MODEL and w_in.shape[2] == D_IN

    qq = jnp.arange(BLOCK)[:, None]
    kk = jnp.arange(3 * BLOCK)[None, :]
    bucket = _t5_bucket(kk - BLOCK - qq).astype(jnp.int32)
    biasm = pl.pallas_call(
        _bias_kernel,
        in_specs=[pl.BlockSpec(memory_space=pltpu.VMEM), pl.BlockSpec(memory_space=pltpu.SMEM)],
        out_specs=pl.BlockSpec(memory_space=pltpu.VMEM),
        out_shape=jax.ShapeDtypeStruct((3, N_Q_HEADS, BLOCK, 3 * BLOCK), _F32),
        name="rel_bias_tables",
    )(bucket, rel_bias.astype(_F32))

    row = lambda a: a.reshape(1, -1).astype(_F32)
    for l in range(depth):
        wcat = jnp.transpose(w_spatial[l], (1, 0, 2)).reshape(BLOCK, N_HEADS_A * BLOCK).astype(_BF16)
        bsp = jnp.repeat(b_spatial[l].T, HEAD_DIM, axis=1).astype(_F32)
        x = _layer_call(
            l == 0, x, row(ln_in_g), row(ln_in_b), w_in[l].astype(_BF16),
            row(gmlp_ln_g[l]), row(gmlp_ln_b[l]), wcat, bsp,
            conv_w[l].astype(_F32), row(conv_b[l]), row(conv_ln_g[l]), row(conv_ln_b[l]),
            attn_sink[l].astype(_F32), biasm, w_out[l].astype(_BF16),
            row(post_ln_g[l]), row(post_ln_b[l]))
    return x
```

```python
import functools
import math

import jax
import jax.numpy as jnp
from jax import lax
from jax.experimental import pallas as pl
from jax.experimental.pallas import tpu as pltpu

D_MODEL = 1024
HEAD_DIM = 64
W_A = 256
W_B = 256
W_C = 512
N_HEADS_A = 4
N_Q_HEADS = 8
N_KV_HEADS = 2
KV_W = N_KV_HEADS * HEAD_DIM
CONV_WIDTH = 31
CONV_PAD = CONV_WIDTH // 2
BLOCK = 128
N_BUCKETS = 32
MAX_DISTANCE = 128
LN_EPS = 1e-5
NEG_INF = -1e30
DEPTH = 2
DEEPNORM_ALPHA = (2 * DEPTH) ** 0.25

COL_A = 0
COL_B = 3 * W_A
COL_Q = COL_B + 3 * W_B
COL_KV = COL_Q + W_C
COL_CG = COL_KV + 2 * KV_W
D_IN = COL_CG + W_C

LANES = 128
SUBLANES = 8
CONV_HALO = 16
CONV_ROWS = 64
GROUP_BLOCKS = 4
PAIR = GROUP_BLOCKS * BLOCK
H_SLOTS = 2
BAND_SLOTS = 4
TILE_ROWS = 512
VMEM_LIMIT_BYTES = 60 * 1024 * 1024

_BF16 = jnp.bfloat16
_F32 = jnp.float32


def _layer_norm(x, g, b):
    mu = jnp.mean(x, axis=-1, keepdims=True)
    xc = x - mu
    var = jnp.mean(xc * xc, axis=-1, keepdims=True)
    return xc * lax.rsqrt(var + LN_EPS) * g + b


def _dot(a, b):
    return jnp.dot(a, b, preferred_element_type=_F32)


def _bias_kernel(bucket_ref, rb_ref, out_ref):
    bucket = bucket_ref[...]
    row = lax.broadcasted_iota(jnp.int32, (BLOCK, 3 * BLOCK), 0)
    col = lax.broadcasted_iota(jnp.int32, (BLOCK, 3 * BLOCK), 1)
    in_window = jnp.abs(col - BLOCK - row) <= BLOCK
    for h in range(N_Q_HEADS):
        acc = jnp.zeros((BLOCK, 3 * BLOCK), _F32)
        for b in range(N_BUCKETS):
            acc = jnp.where(bucket == b, rb_ref[b, h], acc)
        base = jnp.where(in_window, acc, NEG_INF)
        out_ref[0, h] = base
        out_ref[1, h] = jnp.where(col < BLOCK, NEG_INF, base)
        out_ref[2, h] = jnp.where(col >= 2 * BLOCK, NEG_INF, base)


def _layer_kernel(first_layer, tm, nblk_seq,
                  xc_ref, xn_ref, ling_ref, linb_ref, win_ref,
                  gg_ref, gb_ref, wcat_ref, bsp_ref,
                  cw_ref, cb_ref, clg_ref, clb_ref,
                  sink_ref, biasm_ref, wout_ref, pg_ref, pb_ref,
                  out_ref,
                  xb_s, xres_r, ha_r, bg_r, q_r, cg_r, kvar_r, y_r):
    i = pl.program_id(1)
    npair = tm // PAIR
    npair_seq = nblk_seq // GROUP_BLOCKS
    low = lax.broadcasted_iota(jnp.int32, (1, LANES), 1) < HEAD_DIM
    head_of_lane = lax.broadcasted_iota(jnp.int32, (1, W_A), 1) // HEAD_DIM

    def project_pieces(x_pair, p_new):
        hs = p_new & (H_SLOTS - 1)
        bs = p_new & (BAND_SLOTS - 1)
        if first_layer:
            x_pair = _layer_norm(x_pair, ling_ref[...], linb_ref[...])
            xres_r[hs] = x_pair
        xb_s[...] = x_pair.astype(_BF16)

        def piece_a():
            ha_r[hs] = _dot(xb_s[...], win_ref[:, COL_A:COL_B])

        def piece_b():
            hb = _dot(xb_s[...], win_ref[:, COL_B:COL_Q])
            bg_r[hs] = hb[:, 2 * W_B:3 * W_B]
            y = hb[:, 0:W_B] * jax.nn.sigmoid(hb[:, W_B:2 * W_B])
            y_r[bs] = jnp.where(p_new < npair_seq, y, 0.0)

        def piece_q():
            q_r[hs] = (_dot(xb_s[...], win_ref[:, COL_Q:COL_KV]) * (HEAD_DIM ** -0.5)).astype(_BF16)

        def piece_kv():
            kv = _dot(xb_s[...], win_ref[:, COL_KV:COL_CG])
            for t in range(2):
                nat = kv[:, t * KV_W:(t + 1) * KV_W]
                swp = pltpu.roll(nat, HEAD_DIM, axis=1)
                kvar_r[bs, 4 * t + 0] = jnp.where(low, nat, 0.0).astype(_BF16)
                kvar_r[bs, 4 * t + 1] = jnp.where(low, 0.0, swp).astype(_BF16)
                kvar_r[bs, 4 * t + 2] = jnp.where(low, swp, 0.0).astype(_BF16)
                kvar_r[bs, 4 * t + 3] = jnp.where(low, 0.0, nat).astype(_BF16)

        def piece_cg():
            cg_r[hs] = _dot(xb_s[...], win_ref[:, COL_CG:D_IN])

        return [piece_kv, piece_b, piece_a, piece_q, piece_cg]

    def mix_block(b, p, issue):
        hs = p & (H_SLOTS - 1)
        s_here = p & (BAND_SLOTS - 1)
        blk = [slice(k * BLOCK, (k + 1) * BLOCK) for k in range(GROUP_BLOCKS)]
        here = blk[b]
        before = ((p + BAND_SLOTS - 1) & (BAND_SLOTS - 1), blk[-1]) if b == 0 else (s_here, blk[b - 1])
        after = ((p + 1) & (BAND_SLOTS - 1), blk[0]) if b == GROUP_BLOCKS - 1 else (s_here, blk[b + 1])
        band_at = [before, (s_here, here), after]
        g = GROUP_BLOCKS * p + b

        ha = ha_r[hs, here, :]
        u = jax.nn.gelu(ha[:, 0:W_A])
        v = _layer_norm(jax.nn.gelu(ha[:, W_A:2 * W_A]), gg_ref[...], gb_ref[...])
        gate_a = jax.nn.silu(ha[:, 2 * W_A:3 * W_A])
        vb = v.astype(_BF16)
        zero = jnp.zeros_like(vb)
        rhs = jnp.concatenate([jnp.where(head_of_lane == h, vb, zero)
                               for h in range(N_HEADS_A)], axis=0)
        sp = _dot(wcat_ref[...], rhs) + bsp_ref[...]
        ya = (u * sp * gate_a).astype(_BF16)

        (sp_, rp_), (sc_, rc_), (sn_, rn_) = band_at
        ywin = jnp.concatenate([y_r[sp_, rp_.stop - CONV_HALO:rp_.stop, :], y_r[sc_, rc_, :],
                                y_r[sn_, rn_.start:rn_.start + CONV_HALO, :]], axis=0)
        bg = bg_r[hs, here, :]
        span = CONV_ROWS + 2 * CONV_HALO
        yb_parts = []
        for ci in range(BLOCK // CONV_ROWS):
            issue()
            base = ci * CONV_ROWS
            ych = ywin[base:base + span, :]
            acc = jnp.zeros((CONV_ROWS, W_B), _F32)
            for r in range(SUBLANES):
                zr = ych if r == 0 else pltpu.roll(ych, span - r, axis=0)
                for m in range(2 * CONV_HALO // SUBLANES):
                    k = SUBLANES * m + r - 1
                    if 0 <= k < CONV_WIDTH:
                        acc = acc + zr[SUBLANES * m:SUBLANES * m + CONV_ROWS, :] * cw_ref[k:k + 1, :]
            z = _layer_norm(acc + cb_ref[...], clg_ref[...], clb_ref[...])
            gate_b = jax.nn.silu(bg[base:base + CONV_ROWS, :])
            yb_parts.append((jax.nn.silu(z) * gate_b).astype(_BF16))
        yb = jnp.concatenate(yb_parts, axis=0)

        edge = jnp.where(g == 0, 1, jnp.where(g == nblk_seq - 1, 2, 0))
        q_blk = q_r[hs, here, :]
        cg = cg_r[hs, here, :]

        def band(var):
            return [kvar_r[s_, var, r_, :] for (s_, r_) in band_at]

        yc_parts = []
        for kvh in range(N_KV_HEADS):
            issue()
            qg = q_blk[:, kvh * 2 * LANES:(kvh + 1) * 2 * LANES]
            lhs = jnp.concatenate([qg[:, 0:LANES], qg[:, LANES:2 * LANES]], axis=0)
            kcat = jnp.concatenate(band(2 * kvh) + band(2 * kvh + 1), axis=0)
            s2 = lax.dot_general(lhs, kcat, (((1,), (1,)), ((), ())),
                                 preferred_element_type=_F32)
            p_rows = []
            r_inv = []
            for pp in range(2):
                p_pair = []
                for hh in range(2):
                    h = 4 * kvh + 2 * pp + hh
                    s = (s2[pp * BLOCK:(pp + 1) * BLOCK, hh * 3 * BLOCK:(hh + 1) * 3 * BLOCK]
                         + biasm_ref[edge, h])
                    sk = sink_ref[h]
                    m = jnp.maximum(jnp.max(s, axis=-1, keepdims=True), sk)
                    p_ = jnp.exp(s - m)
                    den = jnp.sum(p_, axis=-1, keepdims=True) + jnp.exp(sk - m)
                    r_inv.append(1.0 / den)
                    p_pair.append(p_.astype(_BF16))
                p_rows.append(jnp.concatenate(p_pair, axis=1))
            pcat = jnp.concatenate(p_rows, axis=0)
            vcat = jnp.concatenate(band(4 + 2 * kvh) + band(5 + 2 * kvh), axis=0)
            o2 = _dot(pcat, vcat)
            for pp in range(2):
                cols = slice((2 * kvh + pp) * LANES, (2 * kvh + pp + 1) * LANES)
                scale = jnp.where(low, r_inv[2 * pp], r_inv[2 * pp + 1])
                o = o2[pp * BLOCK:(pp + 1) * BLOCK, :] * scale
                yc_parts.append((o * jax.nn.silu(cg[:, cols])).astype(_BF16))
        return jnp.concatenate([ya, yb] + yc_parts, axis=1)

    def mix(jp, p, pieces):
        pieces = list(pieces)

        def issue(n=1):
            for _ in range(n):
                if pieces:
                    pieces.pop(0)()
        rows = pl.ds(pl.multiple_of(jp * PAIR, PAIR), PAIR)
        hs = p & (H_SLOTS - 1)
        issue(2)
        ymix = jnp.concatenate([mix_block(b, p, issue) for b in range(GROUP_BLOCKS)], axis=0)

        issue(len(pieces))
        y_out = _dot(ymix, wout_ref[...])
        x_res = xres_r[hs] if first_layer else xc_ref[rows, :]
        out_ref[rows, :] = _layer_norm(DEEPNORM_ALPHA * x_res + y_out, pg_ref[...], pb_ref[...])

    p0 = i * npair

    @pl.when(i == 0)
    def _():
        kvar_r[BAND_SLOTS - 1] = jnp.zeros(kvar_r.shape[1:], _BF16)
        y_r[BAND_SLOTS - 1] = jnp.zeros(y_r.shape[1:], _F32)
        for piece in project_pieces(xc_ref[0:PAIR, :], p0):
            piece()

    def body(jp, carry):
        p = p0 + jp
        nxt = pl.multiple_of(jnp.minimum(jp + 1, npair - 1) * PAIR, PAIR)
        x_next = jnp.where(jp + 1 < npair, xc_ref[pl.ds(nxt, PAIR), :], xn_ref[...])
        mix(jp, p, project_pieces(x_next, p + 1))
        return carry

    lax.fori_loop(0, npair, body, 0)


def _t5_bucket(rel):
    nb = N_BUCKETS // 2
    max_exact = nb // 2
    ret = jnp.where(rel > 0, nb, 0)
    n = jnp.abs(rel)
    nf = jnp.maximum(n, 1).astype(jnp.float32)
    large = max_exact + (jnp.log(nf / max_exact) / math.log(MAX_DISTANCE / max_exact)
                         * (nb - max_exact)).astype(jnp.int32)
    large = jnp.minimum(large, nb - 1)
    return ret + jnp.where(n < max_exact, n, large)


def _const_spec(shape):
    zeros = (0,) * len(shape)
    return pl.BlockSpec(shape, lambda b, i: zeros, pipeline_mode=pl.Buffered(1))


def _layer_call(first_layer, x, ling, linb, win, gg, gb, wcat, bsp, cw, cb, clg, clb,
                sink, biasm, wout, pg, pb):
    bsz, seq, d = x.shape
    tm = TILE_ROWS
    assert seq % tm == 0 and seq // BLOCK >= 2
    nb_tile = tm // BLOCK
    nblk_seq = seq // BLOCK
    grid = (bsz, seq // tm)

    in_specs = [
        pl.BlockSpec((None, tm, d), lambda b, i: (b, i, 0)),
        pl.BlockSpec((None, PAIR, d),
                     lambda b, i: (b, jnp.minimum((i + 1) * nb_tile // GROUP_BLOCKS,
                                                  nblk_seq // GROUP_BLOCKS - 1), 0)),
        _const_spec(ling.shape), _const_spec(linb.shape), _const_spec(win.shape),
        _const_spec(gg.shape), _const_spec(gb.shape), _const_spec(wcat.shape),
        _const_spec(bsp.shape),
        _const_spec(cw.shape), _const_spec(cb.shape), _const_spec(clg.shape),
        _const_spec(clb.shape),
        pl.BlockSpec(memory_space=pltpu.SMEM),
        _const_spec(biasm.shape), _const_spec(wout.shape),
        _const_spec(pg.shape), _const_spec(pb.shape),
    ]
    scratch = [
        pltpu.VMEM((PAIR, d), _BF16),
        pltpu.VMEM((H_SLOTS, PAIR if first_layer else SUBLANES, d), _F32),
        pltpu.VMEM((H_SLOTS, PAIR, 3 * W_A), _F32),
        pltpu.VMEM((H_SLOTS, PAIR, W_B), _F32),
        pltpu.VMEM((H_SLOTS, PAIR, W_C), _BF16),
        pltpu.VMEM((H_SLOTS, PAIR, W_C), _F32),
        pltpu.VMEM((BAND_SLOTS, 8, PAIR, LANES), _BF16),
        pltpu.VMEM((BAND_SLOTS, PAIR, W_B), _F32),
    ]
    return pl.pallas_call(
        functools.partial(_layer_kernel, first_layer, tm, nblk_seq),
        grid=grid,
        in_specs=in_specs,
        out_specs=pl.BlockSpec((None, tm, d), lambda b, i: (b, i, 0)),
        out_shape=jax.ShapeDtypeStruct(x.shape, x.dtype),
        scratch_shapes=scratch,
        compiler_params=pltpu.CompilerParams(
            dimension_semantics=("arbitrary", "arbitrary"),
            vmem_limit_bytes=VMEM_LIMIT_BYTES),
        name="layer_first" if first_layer else "layer_next",
    )(x, x, ling, linb, win, gg, gb, wcat, bsp, cw, cb, clg, clb, sink, biasm, wout, pg, pb)


def kernel(x, ln_in_g, ln_in_b, w_in, gmlp_ln_g, gmlp_ln_b, w_spatial, b_spatial, conv_w, conv_b,
           conv_ln_g, conv_ln_b, attn_sink, rel_bias, w_out, post_ln_g, post_ln_b):
    depth = w_in.shape[0]
    assert depth == DEPTH and x.shape[2] == D_MODEL and w_in.shape[2] == D_IN

    qq = jnp.arange(BLOCK)[:, None]
    kk = jnp.arange(3 * BLOCK)[None, :]
    bucket = _t5_bucket(kk - BLOCK - qq).astype(jnp.int32)
    biasm = pl.pallas_call(
        _bias_kernel,
        in_specs=[pl.BlockSpec(memory_space=pltpu.VMEM), pl.BlockSpec(memory_space=pltpu.SMEM)],
        out_specs=pl.BlockSpec(memory_space=pltpu.VMEM),
        out_shape=jax.ShapeDtypeStruct((3, N_Q_HEADS, BLOCK, 3 * BLOCK), _F32),
        name="rel_bias_tables",
    )(bucket, rel_bias.astype(_F32))

    row = lambda a: a.reshape(1, -1).astype(_F32)
    for l in range(depth):
        wcat = jnp.transpose(w_spatial[l], (1, 0, 2)).reshape(BLOCK, N_HEADS_A * BLOCK).astype(_BF16)
        bsp = jnp.repeat(b_spatial[l].T, HEAD_DIM, axis=1).astype(_F32)
        x = _layer_call(
            l == 0, x, row(ln_in_g), row(ln_in_b), w_in[l].astype(_BF16),
            row(gmlp_ln_g[l]), row(gmlp_ln_b[l]), wcat, bsp,
            conv_w[l].astype(_F32), row(conv_b[l]), row(conv_ln_g[l]), row(conv_ln_b[l]),
            attn_sink[l].astype(_F32), biasm, w_out[l].astype(_BF16),
            row(post_ln_g[l]), row(post_ln_b[l]))
    return x
```

```python
import functools
import math

import jax
import jax.numpy as jnp
from jax import lax
from jax.experimental import pallas as pl
from jax.experimental.pallas import tpu as pltpu

D_MODEL = 1024
HEAD_DIM = 64
W_A = 256
W_B = 256
W_C = 512
N_HEADS_A = 4
N_Q_HEADS = 8
N_KV_HEADS = 2
KV_W = N_KV_HEADS * HEAD_DIM
CONV_WIDTH = 31
CONV_PAD = CONV_WIDTH // 2
BLOCK = 128
N_BUCKETS = 32
MAX_DISTANCE = 128
LN_EPS = 1e-5
NEG_INF = -1e30
DEPTH = 2
DEEPNORM_ALPHA = (2 * DEPTH) ** 0.25
LOG2_E = math.log2(math.e)
Q_SCALE = HEAD_DIM ** -0.5 * LOG2_E

COL_A = 0
COL_B = 3 * W_A
COL_Q = COL_B + 3 * W_B
COL_KV = COL_Q + W_C
COL_CG = COL_KV + 2 * KV_W
D_IN = COL_CG + W_C

LANES = 128
SUBLANES = 8
CONV_HALO = 16
CONV_ROWS = 64
GROUP_BLOCKS = 4
PAIR = GROUP_BLOCKS * BLOCK
H_SLOTS = 2
BAND_SLOTS = 4
TILE_ROWS = 512
VMEM_LIMIT_BYTES = 60 * 1024 * 1024

_BF16 = jnp.bfloat16
_F32 = jnp.float32


def _layer_norm(x, g, b):
    mu = jnp.mean(x, axis=-1, keepdims=True)
    xc = x - mu
    var = jnp.mean(xc * xc, axis=-1, keepdims=True)
    return xc * lax.rsqrt(var + LN_EPS) * g + b


def _dot(a, b):
    return jnp.dot(a, b, preferred_element_type=_F32)


def _bias_kernel(bucket_ref, rb_ref, out_ref):
    bucket = bucket_ref[...]
    row = lax.broadcasted_iota(jnp.int32, (BLOCK, 3 * BLOCK), 0)
    col = lax.broadcasted_iota(jnp.int32, (BLOCK, 3 * BLOCK), 1)
    in_window = jnp.abs(col - BLOCK - row) <= BLOCK
    for h in range(N_Q_HEADS):
        acc = jnp.zeros((BLOCK, 3 * BLOCK), _F32)
        for b in range(N_BUCKETS):
            acc = jnp.where(bucket == b, rb_ref[b, h] * LOG2_E, acc)
        base = jnp.where(in_window, acc, NEG_INF)
        out_ref[0, h] = base
        out_ref[1, h] = jnp.where(col < BLOCK, NEG_INF, base)
        out_ref[2, h] = jnp.where(col >= 2 * BLOCK, NEG_INF, base)


def _layer_kernel(first_layer, tm, nblk_seq,
                  xc_ref, xn_ref, ling_ref, linb_ref, win_ref,
                  gg_ref, gb_ref, wcat_ref, bsp_ref,
                  cw_ref, cb_ref, clg_ref, clb_ref,
                  sink_ref, biasm_ref, ones_ref, wout_ref, pg_ref, pb_ref,
                  out_ref,
                  xb_s, xres_r, ha_r, bg_r, q_r, cg_r, kvar_r, y_r):
    i = pl.program_id(1)
    npair = tm // PAIR
    npair_seq = nblk_seq // GROUP_BLOCKS
    low = lax.broadcasted_iota(jnp.int32, (1, LANES), 1) < HEAD_DIM
    head_of_lane = lax.broadcasted_iota(jnp.int32, (1, W_A), 1) // HEAD_DIM

    def project_pieces(x_pair, p_new):
        hs = p_new & (H_SLOTS - 1)
        bs = p_new & (BAND_SLOTS - 1)
        if first_layer:
            x_pair = _layer_norm(x_pair, ling_ref[...], linb_ref[...])
            xres_r[hs] = x_pair
        xb_s[...] = x_pair.astype(_BF16)

        def piece_a():
            ha_r[hs] = _dot(xb_s[...], win_ref[:, COL_A:COL_B])

        def piece_b():
            hb = _dot(xb_s[...], win_ref[:, COL_B:COL_Q])
            bg_r[hs] = hb[:, 2 * W_B:3 * W_B]
            y = hb[:, 0:W_B] * jax.nn.sigmoid(hb[:, W_B:2 * W_B])
            y_r[bs] = jnp.where(p_new < npair_seq, y, 0.0)

        def piece_q():
            q_r[hs] = (_dot(xb_s[...], win_ref[:, COL_Q:COL_KV]) * Q_SCALE).astype(_BF16)

        def piece_kv():
            kv = _dot(xb_s[...], win_ref[:, COL_KV:COL_CG])
            zero = jnp.zeros((PAIR, LANES), _BF16)
            for t in range(2):
                nat = kv[:, t * KV_W:(t + 1) * KV_W]
                swp = pltpu.roll(nat, HEAD_DIM, axis=1).astype(_BF16)
                nat = nat.astype(_BF16)
                kvar_r[bs, 4 * t + 0] = jnp.where(low, nat, zero)
                kvar_r[bs, 4 * t + 1] = jnp.where(low, zero, swp)
                kvar_r[bs, 4 * t + 2] = jnp.where(low, swp, zero)
                kvar_r[bs, 4 * t + 3] = jnp.where(low, zero, nat)

        def piece_cg():
            cg_r[hs] = _dot(xb_s[...], win_ref[:, COL_CG:D_IN])

        return [piece_kv, piece_b, piece_a, piece_q, piece_cg]

    def mix_block(b, p, issue):
        hs = p & (H_SLOTS - 1)
        s_here = p & (BAND_SLOTS - 1)
        blk = [slice(k * BLOCK, (k + 1) * BLOCK) for k in range(GROUP_BLOCKS)]
        here = blk[b]
        before = ((p + BAND_SLOTS - 1) & (BAND_SLOTS - 1), blk[-1]) if b == 0 else (s_here, blk[b - 1])
        after = ((p + 1) & (BAND_SLOTS - 1), blk[0]) if b == GROUP_BLOCKS - 1 else (s_here, blk[b + 1])
        band_at = [before, (s_here, here), after]
        g = GROUP_BLOCKS * p + b

        ha = ha_r[hs, here, :]
        u = jax.nn.gelu(ha[:, 0:W_A])
        v = _layer_norm(jax.nn.gelu(ha[:, W_A:2 * W_A]), gg_ref[...], gb_ref[...])
        gate_a = jax.nn.silu(ha[:, 2 * W_A:3 * W_A])
        vb = v.astype(_BF16)
        zero = jnp.zeros_like(vb)
        rhs = jnp.concatenate([jnp.where(head_of_lane == h, vb, zero)
                               for h in range(N_HEADS_A)], axis=0)
        sp = _dot(wcat_ref[...], rhs) + bsp_ref[...]
        ya = (u * sp * gate_a).astype(_BF16)

        (sp_, rp_), (sc_, rc_), (sn_, rn_) = band_at
        ywin = jnp.concatenate([y_r[sp_, rp_.stop - CONV_HALO:rp_.stop, :], y_r[sc_, rc_, :],
                                y_r[sn_, rn_.start:rn_.start + CONV_HALO, :]], axis=0)
        bg = bg_r[hs, here, :]
        span = CONV_ROWS + 2 * CONV_HALO
        yb_parts = []
        for ci in range(BLOCK // CONV_ROWS):
            issue()
            base = ci * CONV_ROWS
            ych = ywin[base:base + span, :]
            acc = jnp.zeros((CONV_ROWS, W_B), _F32)
            for r in range(SUBLANES):
                zr = ych if r == 0 else pltpu.roll(ych, span - r, axis=0)
                for m in range(2 * CONV_HALO // SUBLANES):
                    k = SUBLANES * m + r - 1
                    if 0 <= k < CONV_WIDTH:
                        acc = acc + zr[SUBLANES * m:SUBLANES * m + CONV_ROWS, :] * cw_ref[k:k + 1, :]
            z = _layer_norm(acc + cb_ref[...], clg_ref[...], clb_ref[...])
            gate_b = jax.nn.silu(bg[base:base + CONV_ROWS, :])
            yb_parts.append((jax.nn.silu(z) * gate_b).astype(_BF16))
        yb = jnp.concatenate(yb_parts, axis=0)

        edge = jnp.where(g == 0, 1, jnp.where(g == nblk_seq - 1, 2, 0))
        q_blk = q_r[hs, here, :]
        cg = cg_r[hs, here, :]

        def band(var):
            return [kvar_r[s_, var, r_, :] for (s_, r_) in band_at]

        yc_parts = []
        for kvh in range(N_KV_HEADS):
            issue()
            qg = q_blk[:, kvh * 2 * LANES:(kvh + 1) * 2 * LANES]
            lhs = jnp.concatenate([qg[:, 0:LANES], qg[:, LANES:2 * LANES]], axis=0)
            kcat = jnp.concatenate(band(2 * kvh) + band(2 * kvh + 1), axis=0)
            s2 = lax.dot_general(lhs, kcat, (((1,), (1,)), ((), ())),
                                 preferred_element_type=_F32)
            p_rows = []
            sink_w = []
            for pp in range(2):
                p_pair = []
                for hh in range(2):
                    h = 4 * kvh + 2 * pp + hh
                    s = (s2[pp * BLOCK:(pp + 1) * BLOCK, hh * 3 * BLOCK:(hh + 1) * 3 * BLOCK]
                         + biasm_ref[edge, h])
                    sk = sink_ref[h] * LOG2_E
                    m = jnp.maximum(jnp.max(s, axis=-1, keepdims=True), sk)
                    p_pair.append(jnp.exp2((s - m).astype(_BF16)))
                    sink_w.append(jnp.exp2(sk - m))
                p_rows.append(jnp.concatenate(p_pair, axis=1))
            pcat = jnp.concatenate(p_rows, axis=0)
            vcat = jnp.concatenate(
                [jnp.concatenate(band(4 + 2 * kvh) + band(5 + 2 * kvh), axis=0), ones_ref[...]], axis=1)
            o2 = _dot(pcat, vcat)
            for pp in range(2):
                rsl = slice(pp * BLOCK, (pp + 1) * BLOCK)
                cols = slice((2 * kvh + pp) * LANES, (2 * kvh + pp + 1) * LANES)
                den_a = o2[rsl, LANES:LANES + 1] + sink_w[2 * pp]
                den_b = o2[rsl, LANES + 1:LANES + 2] + sink_w[2 * pp + 1]
                scale = jnp.where(low, 1.0 / den_a, 1.0 / den_b)
                o = o2[rsl, 0:LANES] * scale
                yc_parts.append((o * jax.nn.silu(cg[:, cols])).astype(_BF16))
        return jnp.concatenate([ya, yb] + yc_parts, axis=1)

    def mix(jp, p, pieces):
        pieces = list(pieces)

        def issue(n=1):
            for _ in range(n):
                if pieces:
                    pieces.pop(0)()
        rows = pl.ds(pl.multiple_of(jp * PAIR, PAIR), PAIR)
        hs = p & (H_SLOTS - 1)
        issue(2)
        ymix = jnp.concatenate([mix_block(b, p, issue) for b in range(GROUP_BLOCKS)], axis=0)

        issue(len(pieces))
        y_out = _dot(ymix, wout_ref[...])
        x_res = xres_r[hs] if first_layer else xc_ref[rows, :]
        out_ref[rows, :] = _layer_norm(DEEPNORM_ALPHA * x_res + y_out, pg_ref[...], pb_ref[...])

    p0 = i * npair

    @pl.when(i == 0)
    def _():
        kvar_r[BAND_SLOTS - 1] = jnp.zeros(kvar_r.shape[1:], _BF16)
        y_r[BAND_SLOTS - 1] = jnp.zeros(y_r.shape[1:], _F32)
        for piece in project_pieces(xc_ref[0:PAIR, :], p0):
            piece()

    def body(jp, carry):
        p = p0 + jp
        if npair == 1:
            x_next = xn_ref[...]
        else:
            nxt = pl.multiple_of(jnp.minimum(jp + 1, npair - 1) * PAIR, PAIR)
            x_next = jnp.where(jp + 1 < npair, xc_ref[pl.ds(nxt, PAIR), :], xn_ref[...])
        mix(jp, p, project_pieces(x_next, p + 1))
        return carry

    lax.fori_loop(0, npair, body, 0)


def _t5_bucket(rel):
    nb = N_BUCKETS // 2
    max_exact = nb // 2
    ret = jnp.where(rel > 0, nb, 0)
    n = jnp.abs(rel)
    nf = jnp.maximum(n, 1).astype(jnp.float32)
    large = max_exact + (jnp.log(nf / max_exact) / math.log(MAX_DISTANCE / max_exact)
                         * (nb - max_exact)).astype(jnp.int32)
    large = jnp.minimum(large, nb - 1)
    return ret + jnp.where(n < max_exact, n, large)


def _const_spec(shape):
    zeros = (0,) * len(shape)
    return pl.BlockSpec(shape, lambda b, i: zeros, pipeline_mode=pl.Buffered(1))


def _layer_call(first_layer, x, ling, linb, win, gg, gb, wcat, bsp, cw, cb, clg, clb,
                sink, biasm, ones, wout, pg, pb):
    bsz, seq, d = x.shape
    tm = TILE_ROWS
    assert seq % tm == 0 and seq // BLOCK >= 2
    nb_tile = tm // BLOCK
    nblk_seq = seq // BLOCK
    grid = (bsz, seq // tm)

    in_specs = [
        pl.BlockSpec((None, tm, d), lambda b, i: (b, i, 0)),
        pl.BlockSpec((None, PAIR, d),
                     lambda b, i: (b, jnp.minimum((i + 1) * nb_tile // GROUP_BLOCKS,
                                                  nblk_seq // GROUP_BLOCKS - 1), 0)),
        _const_spec(ling.shape), _const_spec(linb.shape), _const_spec(win.shape),
        _const_spec(gg.shape), _const_spec(gb.shape), _const_spec(wcat.shape),
        _const_spec(bsp.shape),
        _const_spec(cw.shape), _const_spec(cb.shape), _const_spec(clg.shape),
        _const_spec(clb.shape),
        pl.BlockSpec(memory_space=pltpu.SMEM),
        _const_spec(biasm.shape), _const_spec(ones.shape), _const_spec(wout.shape),
        _const_spec(pg.shape), _const_spec(pb.shape),
    ]
    scratch = [
        pltpu.VMEM((PAIR, d), _BF16),
        pltpu.VMEM((H_SLOTS, PAIR if first_layer else SUBLANES, d), _F32),
        pltpu.VMEM((H_SLOTS, PAIR, 3 * W_A), _F32),
        pltpu.VMEM((H_SLOTS, PAIR, W_B), _F32),
        pltpu.VMEM((H_SLOTS, PAIR, W_C), _BF16),
        pltpu.VMEM((H_SLOTS, PAIR, W_C), _F32),
        pltpu.VMEM((BAND_SLOTS, 8, PAIR, LANES), _BF16),
        pltpu.VMEM((BAND_SLOTS, PAIR, W_B), _F32),
    ]
    return pl.pallas_call(
        functools.partial(_layer_kernel, first_layer, tm, nblk_seq),
        grid=grid,
        in_specs=in_specs,
        out_specs=pl.BlockSpec((None, tm, d), lambda b, i: (b, i, 0)),
        out_shape=jax.ShapeDtypeStruct(x.shape, x.dtype),
        scratch_shapes=scratch,
        compiler_params=pltpu.CompilerParams(
            dimension_semantics=("arbitrary", "arbitrary"),
            vmem_limit_bytes=VMEM_LIMIT_BYTES),
        name="layer_first" if first_layer else "layer_next",
    )(x, x, ling, linb, win, gg, gb, wcat, bsp, cw, cb, clg, clb, sink, biasm, ones, wout, pg, pb)


def kernel(x, ln_in_g, ln_in_b, w_in, gmlp_ln_g, gmlp_ln_b, w_spatial, b_spatial, conv_w, conv_b,
           conv_ln_g, conv_ln_b, attn_sink, rel_bias, w_out, post_ln_g, post_ln_b):
    depth = w_in.shape[0]
    assert depth == DEPTH and x.shape[2] == D_MODEL and w_in.shape[2] == D_IN

    qq = jnp.arange(BLOCK)[:, None]
    kk = jnp.arange(3 * BLOCK)[None, :]
    bucket = _t5_bucket(kk - BLOCK - qq).astype(jnp.int32)
    biasm = pl.pallas_call(
        _bias_kernel,
        in_specs=[pl.BlockSpec(memory_space=pltpu.VMEM), pl.BlockSpec(memory_space=pltpu.SMEM)],
        out_specs=pl.BlockSpec(memory_space=pltpu.VMEM),
        out_shape=jax.ShapeDtypeStruct((3, N_Q_HEADS, BLOCK, 3 * BLOCK), _F32),
        name="rel_bias_tables",
    )(bucket, rel_bias.astype(_F32))

    kidx = jnp.arange(6 * BLOCK)[:, None] // (3 * BLOCK)
    ones = (kidx == jnp.arange(LANES)[None, :]).astype(_BF16)

    row = lambda a: a.reshape(1, -1).astype(_F32)
    for l in range(depth):
        wcat = jnp.transpose(w_spatial[l], (1, 0, 2)).reshape(BLOCK, N_HEADS_A * BLOCK).astype(_BF16)
        bsp = jnp.repeat(b_spatial[l].T, HEAD_DIM, axis=1).astype(_F32)
        x = _layer_call(
            l == 0, x, row(ln_in_g), row(ln_in_b), w_in[l].astype(_BF16),
            row(gmlp_ln_g[l]), row(gmlp_ln_b[l]), wcat, bsp,
            conv_w[l].astype(_F32), row(conv_b[l]), row(conv_ln_g[l]), row(conv_ln_b[l]),
            attn_sink[l].astype(_F32), biasm, ones, w_out[l].astype(_BF16),
            row(post_ln_g[l]), row(post_ln_b[l]))
    return x
```

```python
import functools
import math

import jax
import jax.numpy as jnp
from jax import lax
from jax.experimental import pallas as pl
from jax.experimental.pallas import tpu as pltpu

D_MODEL = 1024
HEAD_DIM = 64
W_A = 256
W_B = 256
W_C = 512
N_HEADS_A = 4
N_Q_HEADS = 8
N_KV_HEADS = 2
KV_W = N_KV_HEADS * HEAD_DIM
CONV_WIDTH = 31
CONV_PAD = CONV_WIDTH // 2
BLOCK = 128
N_BUCKETS = 32
MAX_DISTANCE = 128
LN_EPS = 1e-5
NEG_INF = -1e30
DEPTH = 2
DEEPNORM_ALPHA = (2 * DEPTH) ** 0.25
LOG2_E = math.log2(math.e)
Q_SCALE = HEAD_DIM ** -0.5 * LOG2_E

COL_A = 0
COL_B = 3 * W_A
COL_Q = COL_B + 3 * W_B
COL_KV = COL_Q + W_C
COL_CG = COL_KV + 2 * KV_W
D_IN = COL_CG + W_C

LANES = 128
SUBLANES = 8
CONV_HALO = 16
CONV_ROWS = 64
GROUP_BLOCKS = 4
PAIR = GROUP_BLOCKS * BLOCK
H_SLOTS = 2
BAND_SLOTS = 4
TILE_ROWS = 512
VMEM_LIMIT_BYTES = 60 * 1024 * 1024

_BF16 = jnp.bfloat16
_F32 = jnp.float32


def _layer_norm(x, g, b):
    mu = jnp.mean(x, axis=-1, keepdims=True)
    xc = x - mu
    var = jnp.mean(xc * xc, axis=-1, keepdims=True)
    return xc * lax.rsqrt(var + LN_EPS) * g + b


def _dot(a, b):
    return jnp.dot(a, b, preferred_element_type=_F32)


def _bias_kernel(bucket_ref, rb_ref, out_ref):
    bucket = bucket_ref[...]
    row = lax.broadcasted_iota(jnp.int32, (BLOCK, 3 * BLOCK), 0)
    col = lax.broadcasted_iota(jnp.int32, (BLOCK, 3 * BLOCK), 1)
    in_window = jnp.abs(col - BLOCK - row) <= BLOCK
    for h in range(N_Q_HEADS):
        acc = jnp.zeros((BLOCK, 3 * BLOCK), _F32)
        for b in range(N_BUCKETS):
            acc = jnp.where(bucket == b, rb_ref[b, h] * LOG2_E, acc)
        base = jnp.where(in_window, acc, NEG_INF)
        out_ref[0, h] = base
        out_ref[1, h] = jnp.where(col < BLOCK, NEG_INF, base)
        out_ref[2, h] = jnp.where(col >= 2 * BLOCK, NEG_INF, base)


def _layer_kernel(first_layer, tm, nblk_seq,
                  xc_ref, xn_ref, ling_ref, linb_ref, win_ref,
                  gg_ref, gb_ref, wcat_ref, bsp_ref,
                  cw_ref, cb_ref, clg_ref, clb_ref,
                  sink_ref, biasm_ref, ones_ref, wout_ref, pg_ref, pb_ref,
                  out_ref,
                  xb_s, xres_r, ha_r, bg_r, q_r, cg_r, kvar_r, y_r):
    i = pl.program_id(1)
    npair = tm // PAIR
    npair_seq = nblk_seq // GROUP_BLOCKS
    low = lax.broadcasted_iota(jnp.int32, (1, LANES), 1) < HEAD_DIM
    head_of_lane = lax.broadcasted_iota(jnp.int32, (1, W_A), 1) // HEAD_DIM

    def project_pieces(x_pair, p_new):
        hs = p_new & (H_SLOTS - 1)
        bs = p_new & (BAND_SLOTS - 1)
        if first_layer:
            x_pair = _layer_norm(x_pair, ling_ref[...], linb_ref[...])
            xres_r[hs] = x_pair
        xb_s[...] = x_pair.astype(_BF16)

        def piece_a():
            ha_r[hs] = _dot(xb_s[...], win_ref[:, COL_A:COL_B])

        def piece_b():
            hb = _dot(xb_s[...], win_ref[:, COL_B:COL_Q])
            bg_r[hs] = hb[:, 2 * W_B:3 * W_B]
            y = hb[:, 0:W_B] * jax.nn.sigmoid(hb[:, W_B:2 * W_B])
            y_r[bs] = jnp.where(p_new < npair_seq, y, 0.0)

        def piece_q():
            q_r[hs] = (_dot(xb_s[...], win_ref[:, COL_Q:COL_KV]) * Q_SCALE).astype(_BF16)

        def piece_kv():
            kv = _dot(xb_s[...], win_ref[:, COL_KV:COL_CG])
            zero = jnp.zeros((PAIR, LANES), _BF16)
            for t in range(2):
                nat = kv[:, t * KV_W:(t + 1) * KV_W]
                swp = pltpu.roll(nat, HEAD_DIM, axis=1).astype(_BF16)
                nat = nat.astype(_BF16)
                kvar_r[bs, 4 * t + 0] = jnp.where(low, nat, zero)
                kvar_r[bs, 4 * t + 1] = jnp.where(low, zero, swp)
                kvar_r[bs, 4 * t + 2] = jnp.where(low, swp, zero)
                kvar_r[bs, 4 * t + 3] = jnp.where(low, zero, nat)

        def piece_cg():
            cg_r[hs] = _dot(xb_s[...], win_ref[:, COL_CG:D_IN])

        return [piece_kv, piece_b, piece_a, piece_q, piece_cg]

    def mix_block(b, p, issue):
        hs = p & (H_SLOTS - 1)
        s_here = p & (BAND_SLOTS - 1)
        blk = [slice(k * BLOCK, (k + 1) * BLOCK) for k in range(GROUP_BLOCKS)]
        here = blk[b]
        before = ((p + BAND_SLOTS - 1) & (BAND_SLOTS - 1), blk[-1]) if b == 0 else (s_here, blk[b - 1])
        after = ((p + 1) & (BAND_SLOTS - 1), blk[0]) if b == GROUP_BLOCKS - 1 else (s_here, blk[b + 1])
        band_at = [before, (s_here, here), after]
        g = GROUP_BLOCKS * p + b

        ha = ha_r[hs, here, :]
        u = jax.nn.gelu(ha[:, 0:W_A])
        v = _layer_norm(jax.nn.gelu(ha[:, W_A:2 * W_A]), gg_ref[...], gb_ref[...])
        gate_a = jax.nn.silu(ha[:, 2 * W_A:3 * W_A])
        vb = v.astype(_BF16)
        zero = jnp.zeros_like(vb)
        rhs = jnp.concatenate([jnp.where(head_of_lane == h, vb, zero)
                               for h in range(N_HEADS_A)], axis=0)
        sp = _dot(wcat_ref[...], rhs) + bsp_ref[...]
        ya = (u * sp * gate_a).astype(_BF16)

        (sp_, rp_), (sc_, rc_), (sn_, rn_) = band_at
        ywin = jnp.concatenate([y_r[sp_, rp_.stop - CONV_HALO:rp_.stop, :], y_r[sc_, rc_, :],
                                y_r[sn_, rn_.start:rn_.start + CONV_HALO, :]], axis=0)
        bg = bg_r[hs, here, :]
        span = CONV_ROWS + 2 * CONV_HALO
        yb_parts = []
        for ci in range(BLOCK // CONV_ROWS):
            issue()
            base = ci * CONV_ROWS
            ych = ywin[base:base + span, :]
            acc = jnp.zeros((CONV_ROWS, W_B), _F32)
            for r in range(SUBLANES):
                zr = ych if r == 0 else pltpu.roll(ych, span - r, axis=0)
                for m in range(2 * CONV_HALO // SUBLANES):
                    k = SUBLANES * m + r - 1
                    if 0 <= k < CONV_WIDTH:
                        acc = acc + zr[SUBLANES * m:SUBLANES * m + CONV_ROWS, :] * cw_ref[k:k + 1, :]
            z = _layer_norm(acc + cb_ref[...], clg_ref[...], clb_ref[...])
            gate_b = jax.nn.silu(bg[base:base + CONV_ROWS, :])
            yb_parts.append((jax.nn.silu(z) * gate_b).astype(_BF16))
        yb = jnp.concatenate(yb_parts, axis=0)

        edge = jnp.where(g == 0, 1, jnp.where(g == nblk_seq - 1, 2, 0))
        q_blk = q_r[hs, here, :]
        cg = cg_r[hs, here, :]

        def band(var):
            return [kvar_r[s_, var, r_, :] for (s_, r_) in band_at]

        yc_parts = []
        for kvh in range(N_KV_HEADS):
            issue()
            qg = q_blk[:, kvh * 2 * LANES:(kvh + 1) * 2 * LANES]
            lhs = jnp.concatenate([qg[:, 0:LANES], qg[:, LANES:2 * LANES]], axis=0)
            kcat = jnp.concatenate(band(2 * kvh) + band(2 * kvh + 1), axis=0)
            s2 = lax.dot_general(lhs, kcat, (((1,), (1,)), ((), ())),
                                 preferred_element_type=_F32)
            p_rows = []
            sink_w = []
            for pp in range(2):
                p_pair = []
                for hh in range(2):
                    h = 4 * kvh + 2 * pp + hh
                    s = (s2[pp * BLOCK:(pp + 1) * BLOCK, hh * 3 * BLOCK:(hh + 1) * 3 * BLOCK]
                         + biasm_ref[edge, h])
                    sk = sink_ref[h] * LOG2_E
                    m = jnp.maximum(jnp.max(s, axis=-1, keepdims=True), sk)
                    p_pair.append(jnp.exp2((s - m).astype(_BF16)))
                    sink_w.append(jnp.exp2(sk - m))
                p_rows.append(jnp.concatenate(p_pair, axis=1))
            pcat = jnp.concatenate(p_rows, axis=0)
            vcat = jnp.concatenate(
                [jnp.concatenate(band(4 + 2 * kvh) + band(5 + 2 * kvh), axis=0), ones_ref[...]], axis=1)
            o2 = _dot(pcat, vcat)
            for pp in range(2):
                rsl = slice(pp * BLOCK, (pp + 1) * BLOCK)
                cols = slice((2 * kvh + pp) * LANES, (2 * kvh + pp + 1) * LANES)
                den = o2[rsl, LANES:2 * LANES] + jnp.where(low, sink_w[2 * pp], sink_w[2 * pp + 1])
                o = o2[rsl, 0:LANES] / den
                yc_parts.append((o * jax.nn.silu(cg[:, cols])).astype(_BF16))
        return jnp.concatenate([ya, yb] + yc_parts, axis=1)

    def mix(jp, p, pieces):
        pieces = list(pieces)

        def issue(n=1):
            for _ in range(n):
                if pieces:
                    pieces.pop(0)()
        rows = pl.ds(pl.multiple_of(jp * PAIR, PAIR), PAIR)
        hs = p & (H_SLOTS - 1)
        issue(2)
        ymix = jnp.concatenate([mix_block(b, p, issue) for b in range(GROUP_BLOCKS)], axis=0)

        issue(len(pieces))
        y_out = _dot(ymix, wout_ref[...])
        x_res = xres_r[hs] if first_layer else xc_ref[rows, :]
        out_ref[rows, :] = _layer_norm(DEEPNORM_ALPHA * x_res + y_out, pg_ref[...], pb_ref[...])

    p0 = i * npair

    @pl.when(i == 0)
    def _():
        kvar_r[BAND_SLOTS - 1] = jnp.zeros(kvar_r.shape[1:], _BF16)
        y_r[BAND_SLOTS - 1] = jnp.zeros(y_r.shape[1:], _F32)
        for piece in project_pieces(xc_ref[0:PAIR, :], p0):
            piece()

    def body(jp, carry):
        p = p0 + jp
        if npair == 1:
            x_next = xn_ref[...]
        else:
            nxt = pl.multiple_of(jnp.minimum(jp + 1, npair - 1) * PAIR, PAIR)
            x_next = jnp.where(jp + 1 < npair, xc_ref[pl.ds(nxt, PAIR), :], xn_ref[...])
        mix(jp, p, project_pieces(x_next, p + 1))
        return carry

    lax.fori_loop(0, npair, body, 0)


def _t5_bucket(rel):
    nb = N_BUCKETS // 2
    max_exact = nb // 2
    ret = jnp.where(rel > 0, nb, 0)
    n = jnp.abs(rel)
    nf = jnp.maximum(n, 1).astype(jnp.float32)
    large = max_exact + (jnp.log(nf / max_exact) / math.log(MAX_DISTANCE / max_exact)
                         * (nb - max_exact)).astype(jnp.int32)
    large = jnp.minimum(large, nb - 1)
    return ret + jnp.where(n < max_exact, n, large)


def _const_spec(shape):
    zeros = (0,) * len(shape)
    return pl.BlockSpec(shape, lambda b, i: zeros, pipeline_mode=pl.Buffered(1))


def _layer_call(first_layer, x, ling, linb, win, gg, gb, wcat, bsp, cw, cb, clg, clb,
                sink, biasm, ones, wout, pg, pb):
    bsz, seq, d = x.shape
    tm = TILE_ROWS
    assert seq % tm == 0 and seq // BLOCK >= 2
    nb_tile = tm // BLOCK
    nblk_seq = seq // BLOCK
    grid = (bsz, seq // tm)

    in_specs = [
        pl.BlockSpec((None, tm, d), lambda b, i: (b, i, 0)),
        pl.BlockSpec((None, PAIR, d),
                     lambda b, i: (b, jnp.minimum((i + 1) * nb_tile // GROUP_BLOCKS,
                                                  nblk_seq // GROUP_BLOCKS - 1), 0)),
        _const_spec(ling.shape), _const_spec(linb.shape), _const_spec(win.shape),
        _const_spec(gg.shape), _const_spec(gb.shape), _const_spec(wcat.shape),
        _const_spec(bsp.shape),
        _const_spec(cw.shape), _const_spec(cb.shape), _const_spec(clg.shape),
        _const_spec(clb.shape),
        pl.BlockSpec(memory_space=pltpu.SMEM),
        _const_spec(biasm.shape), _const_spec(ones.shape), _const_spec(wout.shape),
        _const_spec(pg.shape), _const_spec(pb.shape),
    ]
    scratch = [
        pltpu.VMEM((PAIR, d), _BF16),
        pltpu.VMEM((H_SLOTS, PAIR if first_layer else SUBLANES, d), _F32),
        pltpu.VMEM((H_SLOTS, PAIR, 3 * W_A), _F32),
        pltpu.VMEM((H_SLOTS, PAIR, W_B), _F32),
        pltpu.VMEM((H_SLOTS, PAIR, W_C), _BF16),
        pltpu.VMEM((H_SLOTS, PAIR, W_C), _F32),
        pltpu.VMEM((BAND_SLOTS, 8, PAIR, LANES), _BF16),
        pltpu.VMEM((BAND_SLOTS, PAIR, W_B), _F32),
    ]
    return pl.pallas_call(
        functools.partial(_layer_kernel, first_layer, tm, nblk_seq),
        grid=grid,
        in_specs=in_specs,
        out_specs=pl.BlockSpec((None, tm, d), lambda b, i: (b, i, 0)),
        out_shape=jax.ShapeDtypeStruct(x.shape, x.dtype),
        scratch_shapes=scratch,
        compiler_params=pltpu.CompilerParams(
            dimension_semantics=("arbitrary", "arbitrary"),
            vmem_limit_bytes=VMEM_LIMIT_BYTES),
        name="layer_first" if first_layer else "layer_next",
    )(x, x, ling, linb, win, gg, gb, wcat, bsp, cw, cb, clg, clb, sink, biasm, ones, wout, pg, pb)


def kernel(x, ln_in_g, ln_in_b, w_in, gmlp_ln_g, gmlp_ln_b, w_spatial, b_spatial, conv_w, conv_b,
           conv_ln_g, conv_ln_b, attn_sink, rel_bias, w_out, post_ln_g, post_ln_b):
    depth = w_in.shape[0]
    assert depth == DEPTH and x.shape[2] == D_MODEL and w_in.shape[2] == D_IN

    qq = jnp.arange(BLOCK)[:, None]
    kk = jnp.arange(3 * BLOCK)[None, :]
    bucket = _t5_bucket(kk - BLOCK - qq).astype(jnp.int32)
    biasm = pl.pallas_call(
        _bias_kernel,
        in_specs=[pl.BlockSpec(memory_space=pltpu.VMEM), pl.BlockSpec(memory_space=pltpu.SMEM)],
        out_specs=pl.BlockSpec(memory_space=pltpu.VMEM),
        out_shape=jax.ShapeDtypeStruct((3, N_Q_HEADS, BLOCK, 3 * BLOCK), _F32),
        name="rel_bias_tables",
    )(bucket, rel_bias.astype(_F32))

    kidx = jnp.arange(6 * BLOCK)[:, None] // (3 * BLOCK)
    ones = (kidx == jnp.arange(LANES)[None, :] // HEAD_DIM).astype(_BF16)

    row = lambda a: a.reshape(1, -1).astype(_F32)
    for l in range(depth):
        wcat = jnp.transpose(w_spatial[l], (1, 0, 2)).reshape(BLOCK, N_HEADS_A * BLOCK).astype(_BF16)
        bsp = jnp.repeat(b_spatial[l].T, HEAD_DIM, axis=1).astype(_F32)
        x = _layer_call(
            l == 0, x, row(ln_in_g), row(ln_in_b), w_in[l].astype(_BF16),
            row(gmlp_ln_g[l]), row(gmlp_ln_b[l]), wcat, bsp,
            conv_w[l].astype(_F32), row(conv_b[l]), row(conv_ln_g[l]), row(conv_ln_b[l]),
            attn_sink[l].astype(_F32), biasm, ones, w_out[l].astype(_BF16),
            row(post_ln_g[l]), row(post_ln_b[l]))
    return x
```

```python
import functools
import math

import jax
import jax.numpy as jnp
from jax import lax
from jax.experimental import pallas as pl
from jax.experimental.pallas import tpu as pltpu

D_MODEL = 1024
HEAD_DIM = 64
W_A = 256
W_B = 256
W_C = 512
N_HEADS_A = 4
N_Q_HEADS = 8
N_KV_HEADS = 2
KV_W = N_KV_HEADS * HEAD_DIM
CONV_WIDTH = 31
CONV_PAD = CONV_WIDTH // 2
BLOCK = 128
N_BUCKETS = 32
MAX_DISTANCE = 128
LN_EPS = 1e-5
NEG_INF = -1e30
DEPTH = 2
DEEPNORM_ALPHA = (2 * DEPTH) ** 0.25
LOG2_E = math.log2(math.e)
Q_SCALE = HEAD_DIM ** -0.5 * LOG2_E

COL_A = 0
COL_B = 3 * W_A
COL_Q = COL_B + 3 * W_B
COL_KV = COL_Q + W_C
COL_CG = COL_KV + 2 * KV_W
D_IN = COL_CG + W_C

LANES = 128
SUBLANES = 8
CONV_HALO = 16
CONV_ROWS = 64
GROUP_BLOCKS = 4
PAIR = GROUP_BLOCKS * BLOCK
H_SLOTS = 2
BAND_SLOTS = 4
TILE_ROWS = 512
VMEM_LIMIT_BYTES = 60 * 1024 * 1024

_BF16 = jnp.bfloat16
_F32 = jnp.float32


def _layer_norm(x, g, b):
    mu = jnp.mean(x, axis=-1, keepdims=True)
    xc = x - mu
    var = jnp.mean(xc * xc, axis=-1, keepdims=True)
    return xc * lax.rsqrt(var + LN_EPS) * g + b


def _dot(a, b):
    return jnp.dot(a, b, preferred_element_type=_F32)


def _bias_kernel(bucket_ref, rb_ref, out_ref):
    bucket = bucket_ref[...]
    row = lax.broadcasted_iota(jnp.int32, (BLOCK, 3 * BLOCK), 0)
    col = lax.broadcasted_iota(jnp.int32, (BLOCK, 3 * BLOCK), 1)
    in_window = jnp.abs(col - BLOCK - row) <= BLOCK
    for h in range(N_Q_HEADS):
        acc = jnp.zeros((BLOCK, 3 * BLOCK), _F32)
        for b in range(N_BUCKETS):
            acc = jnp.where(bucket == b, rb_ref[b, h] * LOG2_E, acc)
        base = jnp.where(in_window, acc, NEG_INF)
        out_ref[0, h] = base
        out_ref[1, h] = jnp.where(col < BLOCK, NEG_INF, base)
        out_ref[2, h] = jnp.where(col >= 2 * BLOCK, NEG_INF, base)


def _layer_kernel(first_layer, tm, nblk_seq,
                  xc_ref, xn_ref, ling_ref, linb_ref, win_ref,
                  gg_ref, gb_ref, wcat_ref, bsp_ref,
                  cw_ref, cb_ref, clg_ref, clb_ref,
                  sink_ref, biasm_ref, ones_ref, wout_ref, pg_ref, pb_ref,
                  out_ref,
                  xb_s, xres_r, ha_r, bg_r, q_r, cg_r, kvar_r, y_r):
    i = pl.program_id(1)
    npair = tm // PAIR
    npair_seq = nblk_seq // GROUP_BLOCKS
    low = lax.broadcasted_iota(jnp.int32, (1, LANES), 1) < HEAD_DIM
    head_of_lane = lax.broadcasted_iota(jnp.int32, (1, W_A), 1) // HEAD_DIM

    def project_pieces(x_pair, p_new):
        hs = p_new & (H_SLOTS - 1)
        bs = p_new & (BAND_SLOTS - 1)
        if first_layer:
            x_pair = _layer_norm(x_pair, ling_ref[...], linb_ref[...])
            xres_r[hs] = x_pair
        xb_s[...] = x_pair.astype(_BF16)

        def piece_a():
            ha_r[hs] = _dot(xb_s[...], win_ref[:, COL_A:COL_B])

        def piece_b():
            hb = _dot(xb_s[...], win_ref[:, COL_B:COL_Q])
            bg_r[hs] = hb[:, 2 * W_B:3 * W_B]
            y = hb[:, 0:W_B] * jax.nn.sigmoid(hb[:, W_B:2 * W_B])
            y_r[bs] = jnp.where(p_new < npair_seq, y, 0.0)

        def piece_q():
            q_r[hs] = (_dot(xb_s[...], win_ref[:, COL_Q:COL_KV]) * Q_SCALE).astype(_BF16)

        def piece_kv():
            kv = _dot(xb_s[...], win_ref[:, COL_KV:COL_CG])
            zero = jnp.zeros((PAIR, LANES), _BF16)
            for t in range(2):
                nat = kv[:, t * KV_W:(t + 1) * KV_W]
                swp = pltpu.roll(nat, HEAD_DIM, axis=1).astype(_BF16)
                nat = nat.astype(_BF16)
                kvar_r[bs, 4 * t + 0] = jnp.where(low, nat, zero)
                kvar_r[bs, 4 * t + 1] = jnp.where(low, zero, swp)
                kvar_r[bs, 4 * t + 2] = jnp.where(low, swp, zero)
                kvar_r[bs, 4 * t + 3] = jnp.where(low, zero, nat)

        def piece_cg():
            cg_r[hs] = _dot(xb_s[...], win_ref[:, COL_CG:D_IN])

        return [piece_kv, piece_b, piece_a, piece_q, piece_cg]

    def mix_block(b, p, issue):
        hs = p & (H_SLOTS - 1)
        s_here = p & (BAND_SLOTS - 1)
        blk = [slice(k * BLOCK, (k + 1) * BLOCK) for k in range(GROUP_BLOCKS)]
        here = blk[b]
        before = ((p + BAND_SLOTS - 1) & (BAND_SLOTS - 1), blk[-1]) if b == 0 else (s_here, blk[b - 1])
        after = ((p + 1) & (BAND_SLOTS - 1), blk[0]) if b == GROUP_BLOCKS - 1 else (s_here, blk[b + 1])
        band_at = [before, (s_here, here), after]
        g = GROUP_BLOCKS * p + b

        ha = ha_r[hs, here, :]
        u = jax.nn.gelu(ha[:, 0:W_A])
        v = _layer_norm(jax.nn.gelu(ha[:, W_A:2 * W_A]), gg_ref[...], gb_ref[...])
        gate_a = jax.nn.silu(ha[:, 2 * W_A:3 * W_A])
        vb = v.astype(_BF16)
        zero = jnp.zeros_like(vb)
        rhs = jnp.concatenate([jnp.where(head_of_lane == h, vb, zero)
                               for h in range(N_HEADS_A)], axis=0)
        sp = _dot(wcat_ref[...], rhs) + bsp_ref[...]
        ya = (u * sp * gate_a).astype(_BF16)

        (sp_, rp_), (sc_, rc_), (sn_, rn_) = band_at
        ywin = jnp.concatenate([y_r[sp_, rp_.stop - CONV_HALO:rp_.stop, :], y_r[sc_, rc_, :],
                                y_r[sn_, rn_.start:rn_.start + CONV_HALO, :]], axis=0)
        bg = bg_r[hs, here, :]
        span = CONV_ROWS + 2 * CONV_HALO
        yb_parts = []
        for ci in range(BLOCK // CONV_ROWS):
            issue()
            base = ci * CONV_ROWS
            ych = ywin[base:base + span, :]
            acc = jnp.zeros((CONV_ROWS, W_B), _F32)
            for r in range(SUBLANES):
                zr = ych if r == 0 else pltpu.roll(ych, span - r, axis=0)
                for m in range(2 * CONV_HALO // SUBLANES):
                    k = SUBLANES * m + r - 1
                    if 0 <= k < CONV_WIDTH:
                        acc = acc + zr[SUBLANES * m:SUBLANES * m + CONV_ROWS, :] * cw_ref[k:k + 1, :]
            z = _layer_norm(acc + cb_ref[...], clg_ref[...], clb_ref[...])
            gate_b = jax.nn.silu(bg[base:base + CONV_ROWS, :])
            yb_parts.append((jax.nn.silu(z) * gate_b).astype(_BF16))
        yb = jnp.concatenate(yb_parts, axis=0)

        edge = jnp.where(g == 0, 1, jnp.where(g == nblk_seq - 1, 2, 0))
        q_blk = q_r[hs, here, :]
        cg = cg_r[hs, here, :]

        def band(var):
            return [kvar_r[s_, var, r_, :] for (s_, r_) in band_at]

        yc_parts = []
        for kvh in range(N_KV_HEADS):
            issue()
            qg = q_blk[:, kvh * 2 * LANES:(kvh + 1) * 2 * LANES]
            lhs = jnp.concatenate([qg[:, 0:LANES], qg[:, LANES:2 * LANES]], axis=0)
            kcat = jnp.concatenate(band(2 * kvh) + band(2 * kvh + 1), axis=0)
            s2 = lax.dot_general(lhs, kcat, (((1,), (1,)), ((), ())),
                                 preferred_element_type=_F32)
            p_rows = []
            sink_w = []
            for pp in range(2):
                p_pair = []
                for hh in range(2):
                    h = 4 * kvh + 2 * pp + hh
                    s = (s2[pp * BLOCK:(pp + 1) * BLOCK, hh * 3 * BLOCK:(hh + 1) * 3 * BLOCK]
                         + biasm_ref[edge, h])
                    sk = sink_ref[h] * LOG2_E
                    m = jnp.maximum(jnp.max(s, axis=-1, keepdims=True), sk)
                    p_pair.append(jnp.exp2((s - m).astype(_BF16)))
                    sink_w.append(jnp.exp2(sk - m))
                p_rows.append(jnp.concatenate(p_pair, axis=1))
            pcat = jnp.concatenate(p_rows, axis=0)
            vcat = jnp.concatenate(
                [jnp.concatenate(band(4 + 2 * kvh) + band(5 + 2 * kvh), axis=0), ones_ref[...]], axis=1)
            o2 = _dot(pcat, vcat)
            for pp in range(2):
                rsl = slice(pp * BLOCK, (pp + 1) * BLOCK)
                cols = slice((2 * kvh + pp) * LANES, (2 * kvh + pp + 1) * LANES)
                den = o2[rsl, LANES:2 * LANES] + jnp.where(low, sink_w[2 * pp], sink_w[2 * pp + 1])
                o = o2[rsl, 0:LANES] / den
                yc_parts.append((o * jax.nn.silu(cg[:, cols])).astype(_BF16))
        return jnp.concatenate([ya, yb] + yc_parts, axis=1)

    def mix(jp, p, pieces):
        pieces = list(pieces)

        def issue(n=1):
            for _ in range(n):
                if pieces:
                    pieces.pop(0)()
        rows = pl.ds(pl.multiple_of(jp * PAIR, PAIR), PAIR)
        hs = p & (H_SLOTS - 1)
        issue(2)
        ymix = jnp.concatenate([mix_block(b, p, issue) for b in range(GROUP_BLOCKS)], axis=0)

        issue(len(pieces))
        y_out = _dot(ymix, wout_ref[...])
        x_res = xres_r[hs] if first_layer else xc_ref[rows, :]
        out_ref[rows, :] = _layer_norm(DEEPNORM_ALPHA * x_res + y_out, pg_ref[...], pb_ref[...])

    p0 = i * npair

    @pl.when(i == 0)
    def _():
        kvar_r[BAND_SLOTS - 1] = jnp.zeros(kvar_r.shape[1:], _BF16)
        y_r[BAND_SLOTS - 1] = jnp.zeros(y_r.shape[1:], _F32)
        for piece in project_pieces(xc_ref[0:PAIR, :], p0):
            piece()

    def body(jp, carry):
        p = p0 + jp
        if npair == 1:
            x_next = xn_ref[...]
        else:
            nxt = pl.multiple_of(jnp.minimum(jp + 1, npair - 1) * PAIR, PAIR)
            x_next = jnp.where(jp + 1 < npair, xc_ref[pl.ds(nxt, PAIR), :], xn_ref[...])
        mix(jp, p, project_pieces(x_next, p + 1))
        return carry

    lax.fori_loop(0, npair, body, 0)


def _t5_bucket(rel):
    nb = N_BUCKETS // 2
    max_exact = nb // 2
    ret = jnp.where(rel > 0, nb, 0)
    n = jnp.abs(rel)
    nf = jnp.maximum(n, 1).astype(jnp.float32)
    large = max_exact + (jnp.log(nf / max_exact) / math.log(MAX_DISTANCE / max_exact)
                         * (nb - max_exact)).astype(jnp.int32)
    large = jnp.minimum(large, nb - 1)
    return ret + jnp.where(n < max_exact, n, large)


def _const_spec(shape):
    zeros = (0,) * len(shape)
    return pl.BlockSpec(shape, lambda b, i: zeros, pipeline_mode=pl.Buffered(1))


def _layer_slice_spec(stacked, layer):
    return pl.BlockSpec((None,) + stacked.shape[1:], lambda b, i: (layer, 0, 0),
                        pipeline_mode=pl.Buffered(1))


def _layer_call(layer, x, ling, linb, win, gg, gb, wcat, bsp, cw, cb, clg, clb,
                sink, biasm, ones, wout, pg, pb):
    first_layer = layer == 0
    bsz, seq, d = x.shape
    tm = TILE_ROWS
    assert seq % tm == 0 and seq // BLOCK >= 2
    nb_tile = tm // BLOCK
    nblk_seq = seq // BLOCK
    grid = (bsz, seq // tm)

    in_specs = [
        pl.BlockSpec((None, tm, d), lambda b, i: (b, i, 0)),
        pl.BlockSpec((None, PAIR, d),
                     lambda b, i: (b, jnp.minimum((i + 1) * nb_tile // GROUP_BLOCKS,
                                                  nblk_seq // GROUP_BLOCKS - 1), 0)),
        _const_spec(ling.shape), _const_spec(linb.shape), _layer_slice_spec(win, layer),
        _const_spec(gg.shape), _const_spec(gb.shape), _const_spec(wcat.shape),
        _const_spec(bsp.shape),
        _const_spec(cw.shape), _const_spec(cb.shape), _const_spec(clg.shape),
        _const_spec(clb.shape),
        pl.BlockSpec(memory_space=pltpu.SMEM),
        _const_spec(biasm.shape), _const_spec(ones.shape), _layer_slice_spec(wout, layer),
        _const_spec(pg.shape), _const_spec(pb.shape),
    ]
    scratch = [
        pltpu.VMEM((PAIR, d), _BF16),
        pltpu.VMEM((H_SLOTS, PAIR if first_layer else SUBLANES, d), _F32),
        pltpu.VMEM((H_SLOTS, PAIR, 3 * W_A), _F32),
        pltpu.VMEM((H_SLOTS, PAIR, W_B), _F32),
        pltpu.VMEM((H_SLOTS, PAIR, W_C), _BF16),
        pltpu.VMEM((H_SLOTS, PAIR, W_C), _F32),
        pltpu.VMEM((BAND_SLOTS, 8, PAIR, LANES), _BF16),
        pltpu.VMEM((BAND_SLOTS, PAIR, W_B), _F32),
    ]
    return pl.pallas_call(
        functools.partial(_layer_kernel, first_layer, tm, nblk_seq),
        grid=grid,
        in_specs=in_specs,
        out_specs=pl.BlockSpec((None, tm, d), lambda b, i: (b, i, 0)),
        out_shape=jax.ShapeDtypeStruct(x.shape, x.dtype),
        scratch_shapes=scratch,
        compiler_params=pltpu.CompilerParams(
            dimension_semantics=("arbitrary", "arbitrary"),
            vmem_limit_bytes=VMEM_LIMIT_BYTES),
        name="layer_first" if first_layer else "layer_next",
    )(x, x, ling, linb, win, gg, gb, wcat, bsp, cw, cb, clg, clb, sink, biasm, ones, wout, pg, pb)


def kernel(x, ln_in_g, ln_in_b, w_in, gmlp_ln_g, gmlp_ln_b, w_spatial, b_spatial, conv_w, conv_b,
           conv_ln_g, conv_ln_b, attn_sink, rel_bias, w_out, post_ln_g, post_ln_b):
    depth = w_in.shape[0]
    assert depth == DEPTH and x.shape[2] == D_MODEL and w_in.shape[2] == D_IN

    qq = jnp.arange(BLOCK)[:, None]
    kk = jnp.arange(3 * BLOCK)[None, :]
    bucket = _t5_bucket(kk - BLOCK - qq).astype(jnp.int32)
    biasm = pl.pallas_call(
        _bias_kernel,
        in_specs=[pl.BlockSpec(memory_space=pltpu.VMEM), pl.BlockSpec(memory_space=pltpu.SMEM)],
        out_specs=pl.BlockSpec(memory_space=pltpu.VMEM),
        out_shape=jax.ShapeDtypeStruct((3, N_Q_HEADS, BLOCK, 3 * BLOCK), _F32),
        name="rel_bias_tables",
    )(bucket, rel_bias.astype(_F32))

    kidx = jnp.arange(6 * BLOCK)[:, None] // (3 * BLOCK)
    ones = (kidx == jnp.arange(LANES)[None, :] // HEAD_DIM).astype(_BF16)

    row = lambda a: a.reshape(1, -1).astype(_F32)
    win_all = w_in.astype(_BF16)
    wout_all = w_out.astype(_BF16)
    for l in range(depth):
        wcat = jnp.transpose(w_spatial[l], (1, 0, 2)).reshape(BLOCK, N_HEADS_A * BLOCK).astype(_BF16)
        bsp = jnp.repeat(b_spatial[l].T, HEAD_DIM, axis=1).astype(_F32)
        x = _layer_call(
            l, x, row(ln_in_g), row(ln_in_b), win_all,
            row(gmlp_ln_g[l]), row(gmlp_ln_b[l]), wcat, bsp,
            conv_w[l].astype(_F32), row(conv_b[l]), row(conv_ln_g[l]), row(conv_ln_b[l]),
            attn_sink[l].astype(_F32), biasm, ones, wout_all,
            row(post_ln_g[l]), row(post_ln_b[l]))
    return x
```

```python
import functools
import math

import jax
import jax.numpy as jnp
from jax import lax
from jax.experimental import pallas as pl
from jax.experimental.pallas import tpu as pltpu

D_MODEL = 1024
HEAD_DIM = 64
W_A = 256
W_B = 256
W_C = 512
N_HEADS_A = 4
N_Q_HEADS = 8
N_KV_HEADS = 2
KV_W = N_KV_HEADS * HEAD_DIM
CONV_WIDTH = 31
CONV_PAD = CONV_WIDTH // 2
BLOCK = 128
N_BUCKETS = 32
MAX_DISTANCE = 128
LN_EPS = 1e-5
NEG_INF = -1e30
DEPTH = 2
DEEPNORM_ALPHA = (2 * DEPTH) ** 0.25
LOG2_E = math.log2(math.e)
Q_SCALE = HEAD_DIM ** -0.5 * LOG2_E

COL_A = 0
COL_B = 3 * W_A
COL_Q = COL_B + 3 * W_B
COL_KV = COL_Q + W_C
COL_CG = COL_KV + 2 * KV_W
D_IN = COL_CG + W_C

LANES = 128
SUBLANES = 8
CONV_HALO = 16
CONV_ROWS = 64
GROUP_BLOCKS = 4
PAIR = GROUP_BLOCKS * BLOCK
H_SLOTS = 2
BAND_SLOTS = 4
TILE_ROWS = 512
VMEM_LIMIT_BYTES = 60 * 1024 * 1024

_BF16 = jnp.bfloat16
_F32 = jnp.float32


def _layer_norm(x, g, b):
    mu = jnp.mean(x, axis=-1, keepdims=True)
    xc = x - mu
    var = jnp.mean(xc * xc, axis=-1, keepdims=True)
    return xc * lax.rsqrt(var + LN_EPS) * g + b


def _dot(a, b):
    return jnp.dot(a, b, preferred_element_type=_F32)


def _bias_kernel(bucket_ref, rb_ref, out_ref):
    bucket = bucket_ref[...]
    row = lax.broadcasted_iota(jnp.int32, (BLOCK, 3 * BLOCK), 0)
    col = lax.broadcasted_iota(jnp.int32, (BLOCK, 3 * BLOCK), 1)
    in_window = jnp.abs(col - BLOCK - row) <= BLOCK
    for h in range(N_Q_HEADS):
        acc = jnp.zeros((BLOCK, 3 * BLOCK), _F32)
        for b in range(N_BUCKETS):
            acc = jnp.where(bucket == b, rb_ref[b, h] * LOG2_E, acc)
        base = jnp.where(in_window, acc, NEG_INF)
        out_ref[0, h] = base
        out_ref[1, h] = jnp.where(col < BLOCK, NEG_INF, base)
        out_ref[2, h] = jnp.where(col >= 2 * BLOCK, NEG_INF, base)


def _layer_kernel(first_layer, tm, nblk_seq,
                  xc_ref, xn_ref, ling_ref, linb_ref, win_ref,
                  gg_ref, gb_ref, wcat_ref, bsp_ref,
                  cw_ref, cb_ref, clg_ref, clb_ref,
                  sink_ref, biasm_ref, ones_ref, wout_ref, pg_ref, pb_ref,
                  out_ref,
                  xb_s, xres_r, u_r, ga_r, rhs_r, bg_r, q_r, cg_r, kvar_r, y_r):
    i = pl.program_id(1)
    npair = tm // PAIR
    npair_seq = nblk_seq // GROUP_BLOCKS
    low = lax.broadcasted_iota(jnp.int32, (1, LANES), 1) < HEAD_DIM
    head_of_lane = lax.broadcasted_iota(jnp.int32, (1, W_A), 1) // HEAD_DIM

    def project_pieces(x_pair, p_new):
        hs = p_new & (H_SLOTS - 1)
        bs = p_new & (BAND_SLOTS - 1)
        if first_layer:
            x_pair = _layer_norm(x_pair, ling_ref[...], linb_ref[...])
            xres_r[hs] = x_pair
        xb_s[...] = x_pair.astype(_BF16)

        def piece_a():
            ha = _dot(xb_s[...], win_ref[:, COL_A:COL_B])
            u_r[hs] = jax.nn.gelu(ha[:, 0:W_A])
            ga_r[hs] = jax.nn.silu(ha[:, 2 * W_A:3 * W_A])
            v = _layer_norm(jax.nn.gelu(ha[:, W_A:2 * W_A]), gg_ref[...], gb_ref[...])
            vb = v.astype(_BF16)
            zero = jnp.zeros_like(vb)
            for h in range(N_HEADS_A):
                rhs_r[hs, h] = jnp.where(head_of_lane == h, vb, zero)

        def piece_b():
            hb = _dot(xb_s[...], win_ref[:, COL_B:COL_Q])
            bg_r[hs] = jax.nn.silu(hb[:, 2 * W_B:3 * W_B])
            y = hb[:, 0:W_B] * jax.nn.sigmoid(hb[:, W_B:2 * W_B])
            y_r[bs] = jnp.where(p_new < npair_seq, y, 0.0)

        def piece_q():
            q_r[hs] = (_dot(xb_s[...], win_ref[:, COL_Q:COL_KV]) * Q_SCALE).astype(_BF16)

        def piece_kv():
            kv = _dot(xb_s[...], win_ref[:, COL_KV:COL_CG])
            zero = jnp.zeros((PAIR, LANES), _BF16)
            for t in range(2):
                nat = kv[:, t * KV_W:(t + 1) * KV_W]
                swp = pltpu.roll(nat, HEAD_DIM, axis=1).astype(_BF16)
                nat = nat.astype(_BF16)
                kvar_r[bs, 4 * t + 0] = jnp.where(low, nat, zero)
                kvar_r[bs, 4 * t + 1] = jnp.where(low, zero, swp)
                kvar_r[bs, 4 * t + 2] = jnp.where(low, swp, zero)
                kvar_r[bs, 4 * t + 3] = jnp.where(low, zero, nat)

        def piece_cg():
            cg_r[hs] = jax.nn.silu(_dot(xb_s[...], win_ref[:, COL_CG:D_IN]))

        return [piece_kv, piece_b, piece_a, piece_q, piece_cg]

    def mix_block(b, p, issue):
        hs = p & (H_SLOTS - 1)
        s_here = p & (BAND_SLOTS - 1)
        blk = [slice(k * BLOCK, (k + 1) * BLOCK) for k in range(GROUP_BLOCKS)]
        here = blk[b]
        before = ((p + BAND_SLOTS - 1) & (BAND_SLOTS - 1), blk[-1]) if b == 0 else (s_here, blk[b - 1])
        after = ((p + 1) & (BAND_SLOTS - 1), blk[0]) if b == GROUP_BLOCKS - 1 else (s_here, blk[b + 1])
        band_at = [before, (s_here, here), after]
        g = GROUP_BLOCKS * p + b

        rhs = jnp.concatenate([rhs_r[hs, h, here, :] for h in range(N_HEADS_A)], axis=0)
        sp = _dot(wcat_ref[...], rhs) + bsp_ref[...]
        ya = (u_r[hs, here, :] * sp * ga_r[hs, here, :]).astype(_BF16)

        (sp_, rp_), (sc_, rc_), (sn_, rn_) = band_at
        ywin = jnp.concatenate([y_r[sp_, rp_.stop - CONV_HALO:rp_.stop, :], y_r[sc_, rc_, :],
                                y_r[sn_, rn_.start:rn_.start + CONV_HALO, :]], axis=0)
        bg = bg_r[hs, here, :]
        span = CONV_ROWS + 2 * CONV_HALO
        yb_parts = []
        for ci in range(BLOCK // CONV_ROWS):
            issue()
            base = ci * CONV_ROWS
            ych = ywin[base:base + span, :]
            acc = jnp.zeros((CONV_ROWS, W_B), _F32)
            for r in range(SUBLANES):
                zr = ych if r == 0 else pltpu.roll(ych, span - r, axis=0)
                for m in range(2 * CONV_HALO // SUBLANES):
                    k = SUBLANES * m + r - 1
                    if 0 <= k < CONV_WIDTH:
                        acc = acc + zr[SUBLANES * m:SUBLANES * m + CONV_ROWS, :] * cw_ref[k:k + 1, :]
            z = _layer_norm(acc + cb_ref[...], clg_ref[...], clb_ref[...])
            gate_b = bg[base:base + CONV_ROWS, :]
            yb_parts.append((jax.nn.silu(z) * gate_b).astype(_BF16))
        yb = jnp.concatenate(yb_parts, axis=0)

        edge = jnp.where(g == 0, 1, jnp.where(g == nblk_seq - 1, 2, 0))
        q_blk = q_r[hs, here, :]
        cg = cg_r[hs, here, :]

        def band(var):
            return [kvar_r[s_, var, r_, :] for (s_, r_) in band_at]

        yc_parts = []
        for kvh in range(N_KV_HEADS):
            issue()
            qg = q_blk[:, kvh * 2 * LANES:(kvh + 1) * 2 * LANES]
            lhs = jnp.concatenate([qg[:, 0:LANES], qg[:, LANES:2 * LANES]], axis=0)
            kcat = jnp.concatenate(band(2 * kvh) + band(2 * kvh + 1), axis=0)
            s2 = lax.dot_general(lhs, kcat, (((1,), (1,)), ((), ())),
                                 preferred_element_type=_F32)
            p_rows = []
            sink_w = []
            for pp in range(2):
                p_pair = []
                for hh in range(2):
                    h = 4 * kvh + 2 * pp + hh
                    s = (s2[pp * BLOCK:(pp + 1) * BLOCK, hh * 3 * BLOCK:(hh + 1) * 3 * BLOCK]
                         + biasm_ref[edge, h])
                    sk = sink_ref[h] * LOG2_E
                    m = jnp.maximum(jnp.max(s, axis=-1, keepdims=True), sk)
                    p_pair.append(jnp.exp2((s - m).astype(_BF16)))
                    sink_w.append(jnp.exp2(sk - m))
                p_rows.append(jnp.concatenate(p_pair, axis=1))
            pcat = jnp.concatenate(p_rows, axis=0)
            vcat = jnp.concatenate(
                [jnp.concatenate(band(4 + 2 * kvh) + band(5 + 2 * kvh), axis=0), ones_ref[...]], axis=1)
            o2 = _dot(pcat, vcat)
            for pp in range(2):
                rsl = slice(pp * BLOCK, (pp + 1) * BLOCK)
                cols = slice((2 * kvh + pp) * LANES, (2 * kvh + pp + 1) * LANES)
                den = o2[rsl, LANES:2 * LANES] + jnp.where(low, sink_w[2 * pp], sink_w[2 * pp + 1])
                o = o2[rsl, 0:LANES] / den
                yc_parts.append((o * cg[:, cols]).astype(_BF16))
        return jnp.concatenate([ya, yb] + yc_parts, axis=1)

    def mix(jp, p, pieces):
        pieces = list(pieces)

        def issue(n=1):
            for _ in range(n):
                if pieces:
                    pieces.pop(0)()
        rows = pl.ds(pl.multiple_of(jp * PAIR, PAIR), PAIR)
        hs = p & (H_SLOTS - 1)
        issue(2)
        ymix = jnp.concatenate([mix_block(b, p, issue) for b in range(GROUP_BLOCKS)], axis=0)

        issue(len(pieces))
        y_out = _dot(ymix, wout_ref[...])
        x_res = xres_r[hs] if first_layer else xc_ref[rows, :]
        out_ref[rows, :] = _layer_norm(DEEPNORM_ALPHA * x_res + y_out, pg_ref[...], pb_ref[...])

    p0 = i * npair

    @pl.when(i == 0)
    def _():
        kvar_r[BAND_SLOTS - 1] = jnp.zeros(kvar_r.shape[1:], _BF16)
        y_r[BAND_SLOTS - 1] = jnp.zeros(y_r.shape[1:], _F32)
        for piece in project_pieces(xc_ref[0:PAIR, :], p0):
            piece()

    def body(jp, carry):
        p = p0 + jp
        if npair == 1:
            x_next = xn_ref[...]
        else:
            nxt = pl.multiple_of(jnp.minimum(jp + 1, npair - 1) * PAIR, PAIR)
            x_next = jnp.where(jp + 1 < npair, xc_ref[pl.ds(nxt, PAIR), :], xn_ref[...])
        mix(jp, p, project_pieces(x_next, p + 1))
        return carry

    lax.fori_loop(0, npair, body, 0)


def _t5_bucket(rel):
    nb = N_BUCKETS // 2
    max_exact = nb // 2
    ret = jnp.where(rel > 0, nb, 0)
    n = jnp.abs(rel)
    nf = jnp.maximum(n, 1).astype(jnp.float32)
    large = max_exact + (jnp.log(nf / max_exact) / math.log(MAX_DISTANCE / max_exact)
                         * (nb - max_exact)).astype(jnp.int32)
    large = jnp.minimum(large, nb - 1)
    return ret + jnp.where(n < max_exact, n, large)


def _const_spec(shape):
    zeros = (0,) * len(shape)
    return pl.BlockSpec(shape, lambda b, i: zeros, pipeline_mode=pl.Buffered(1))


def _layer_slice_spec(stacked, layer):
    return pl.BlockSpec((None,) + stacked.shape[1:], lambda b, i: (layer, 0, 0),
                        pipeline_mode=pl.Buffered(1))


def _layer_call(layer, x, ling, linb, win, gg, gb, wcat, bsp, cw, cb, clg, clb,
                sink, biasm, ones, wout, pg, pb):
    first_layer = layer == 0
    bsz, seq, d = x.shape
    tm = TILE_ROWS
    assert seq % tm == 0 and seq // BLOCK >= 2
    nb_tile = tm // BLOCK
    nblk_seq = seq // BLOCK
    grid = (bsz, seq // tm)

    in_specs = [
        pl.BlockSpec((None, tm, d), lambda b, i: (b, i, 0)),
        pl.BlockSpec((None, PAIR, d),
                     lambda b, i: (b, jnp.minimum((i + 1) * nb_tile // GROUP_BLOCKS,
                                                  nblk_seq // GROUP_BLOCKS - 1), 0)),
        _const_spec(ling.shape), _const_spec(linb.shape), _layer_slice_spec(win, layer),
        _const_spec(gg.shape), _const_spec(gb.shape), _const_spec(wcat.shape),
        _const_spec(bsp.shape),
        _const_spec(cw.shape), _const_spec(cb.shape), _const_spec(clg.shape),
        _const_spec(clb.shape),
        pl.BlockSpec(memory_space=pltpu.SMEM),
        _const_spec(biasm.shape), _const_spec(ones.shape), _layer_slice_spec(wout, layer),
        _const_spec(pg.shape), _const_spec(pb.shape),
    ]
    scratch = [
        pltpu.VMEM((PAIR, d), _BF16),
        pltpu.VMEM((H_SLOTS, PAIR if first_layer else SUBLANES, d), _F32),
        pltpu.VMEM((H_SLOTS, PAIR, W_A), _F32),
        pltpu.VMEM((H_SLOTS, PAIR, W_A), _F32),
        pltpu.VMEM((H_SLOTS, N_HEADS_A, PAIR, W_A), _BF16),
        pltpu.VMEM((H_SLOTS, PAIR, W_B), _F32),
        pltpu.VMEM((H_SLOTS, PAIR, W_C), _BF16),
        pltpu.VMEM((H_SLOTS, PAIR, W_C), _F32),
        pltpu.VMEM((BAND_SLOTS, 8, PAIR, LANES), _BF16),
        pltpu.VMEM((BAND_SLOTS, PAIR, W_B), _F32),
    ]
    return pl.pallas_call(
        functools.partial(_layer_kernel, first_layer, tm, nblk_seq),
        grid=grid,
        in_specs=in_specs,
        out_specs=pl.BlockSpec((None, tm, d), lambda b, i: (b, i, 0)),
        out_shape=jax.ShapeDtypeStruct(x.shape, x.dtype),
        scratch_shapes=scratch,
        compiler_params=pltpu.CompilerParams(
            dimension_semantics=("arbitrary", "arbitrary"),
            vmem_limit_bytes=VMEM_LIMIT_BYTES),
        name="layer_first" if first_layer else "layer_next",
    )(x, x, ling, linb, win, gg, gb, wcat, bsp, cw, cb, clg, clb, sink, biasm, ones, wout, pg, pb)


def kernel(x, ln_in_g, ln_in_b, w_in, gmlp_ln_g, gmlp_ln_b, w_spatial, b_spatial, conv_w, conv_b,
           conv_ln_g, conv_ln_b, attn_sink, rel_bias, w_out, post_ln_g, post_ln_b):
    depth = w_in.shape[0]
    assert depth == DEPTH and x.shape[2] == D_MODEL and w_in.shape[2] == D_IN

    qq = jnp.arange(BLOCK)[:, None]
    kk = jnp.arange(3 * BLOCK)[None, :]
    bucket = _t5_bucket(kk - BLOCK - qq).astype(jnp.int32)
    biasm = pl.pallas_call(
        _bias_kernel,
        in_specs=[pl.BlockSpec(memory_space=pltpu.VMEM), pl.BlockSpec(memory_space=pltpu.SMEM)],
        out_specs=pl.BlockSpec(memory_space=pltpu.VMEM),
        out_shape=jax.ShapeDtypeStruct((3, N_Q_HEADS, BLOCK, 3 * BLOCK), _F32),
        name="rel_bias_tables",
    )(bucket, rel_bias.astype(_F32))

    kidx = jnp.arange(6 * BLOCK)[:, None] // (3 * BLOCK)
    ones = (kidx == jnp.arange(LANES)[None, :] // HEAD_DIM).astype(_BF16)

    row = lambda a: a.reshape(1, -1).astype(_F32)
    win_all = w_in.astype(_BF16)
    wout_all = w_out.astype(_BF16)
    for l in range(depth):
        wcat = jnp.transpose(w_spatial[l], (1, 0, 2)).reshape(BLOCK, N_HEADS_A * BLOCK).astype(_BF16)
        bsp = jnp.repeat(b_spatial[l].T, HEAD_DIM, axis=1).astype(_F32)
        x = _layer_call(
            l, x, row(ln_in_g), row(ln_in_b), win_all,
            row(gmlp_ln_g[l]), row(gmlp_ln_b[l]), wcat, bsp,
            conv_w[l].astype(_F32), row(conv_b[l]), row(conv_ln_g[l]), row(conv_ln_b[l]),
            attn_sink[l].astype(_F32), biasm, ones, wout_all,
            row(post_ln_g[l]), row(post_ln_b[l]))
    return x
```

```python
import functools
import math

import jax
import jax.numpy as jnp
from jax import lax
from jax.experimental import pallas as pl
from jax.experimental.pallas import tpu as pltpu

D_MODEL = 1024
HEAD_DIM = 64
W_A = 256
W_B = 256
W_C = 512
N_HEADS_A = 4
N_Q_HEADS = 8
N_KV_HEADS = 2
KV_W = N_KV_HEADS * HEAD_DIM
CONV_WIDTH = 31
CONV_PAD = CONV_WIDTH // 2
BLOCK = 128
N_BUCKETS = 32
MAX_DISTANCE = 128
LN_EPS = 1e-5
NEG_INF = -1e30
DEPTH = 2
DEEPNORM_ALPHA = (2 * DEPTH) ** 0.25
LOG2_E = math.log2(math.e)
Q_SCALE = HEAD_DIM ** -0.5 * LOG2_E

COL_A = 0
COL_B = 3 * W_A
COL_Q = COL_B + 3 * W_B
COL_KV = COL_Q + W_C
COL_CG = COL_KV + 2 * KV_W
D_IN = COL_CG + W_C

LANES = 128
SUBLANES = 8
CONV_HALO = 16
CONV_ROWS = 64
GROUP_BLOCKS = 4
GROUP_ROWS = GROUP_BLOCKS * BLOCK
H_SLOTS = 2
BAND_SLOTS = 4
TILE_ROWS = GROUP_ROWS
VMEM_LIMIT_BYTES = 60 * 1024 * 1024

_BF16 = jnp.bfloat16
_F32 = jnp.float32


def _layer_norm(x, g, b):
    mu = jnp.mean(x, axis=-1, keepdims=True)
    xc = x - mu
    var = jnp.mean(xc * xc, axis=-1, keepdims=True)
    return xc * lax.rsqrt(var + LN_EPS) * g + b


def _dot(a, b):
    return jnp.dot(a, b, preferred_element_type=_F32)


def _bias_kernel(bucket_ref, rb_ref, out_ref):
    bucket = bucket_ref[...]
    row = lax.broadcasted_iota(jnp.int32, (BLOCK, 3 * BLOCK), 0)
    col = lax.broadcasted_iota(jnp.int32, (BLOCK, 3 * BLOCK), 1)
    in_window = jnp.abs(col - BLOCK - row) <= BLOCK
    for h in range(N_Q_HEADS):
        acc = jnp.zeros((BLOCK, 3 * BLOCK), _F32)
        for b in range(N_BUCKETS):
            acc = jnp.where(bucket == b, rb_ref[b, h] * LOG2_E, acc)
        base = jnp.where(in_window, acc, NEG_INF)
        out_ref[0, h] = base
        out_ref[1, h] = jnp.where(col < BLOCK, NEG_INF, base)
        out_ref[2, h] = jnp.where(col >= 2 * BLOCK, NEG_INF, base)


def _layer_kernel(first_layer, tm, nblk_seq,
                  xc_ref, xn_ref, ling_ref, linb_ref, win_ref,
                  gg_ref, gb_ref, wcat_ref, bsp_ref,
                  cw_ref, cb_ref, clg_ref, clb_ref,
                  sink_ref, biasm_ref, ones_ref, wout_ref, pg_ref, pb_ref,
                  out_ref,
                  xb_s, xres_r, ha_r, bg_r, q_r, cg_r, kvar_r, y_r):
    i = pl.program_id(1)
    n_groups = tm // GROUP_ROWS
    n_groups_seq = nblk_seq // GROUP_BLOCKS
    low = lax.broadcasted_iota(jnp.int32, (1, LANES), 1) < HEAD_DIM
    head_of_lane = lax.broadcasted_iota(jnp.int32, (1, W_A), 1) // HEAD_DIM

    def project_pieces(x_grp, p_new):
        hs = p_new & (H_SLOTS - 1)
        bs = p_new & (BAND_SLOTS - 1)
        if first_layer:
            x_grp = _layer_norm(x_grp, ling_ref[...], linb_ref[...])
            xres_r[hs] = x_grp
        xb_s[...] = x_grp.astype(_BF16)

        def piece_a():
            ha_r[hs] = _dot(xb_s[...], win_ref[:, COL_A:COL_B])

        def piece_b():
            hb = _dot(xb_s[...], win_ref[:, COL_B:COL_Q])
            bg_r[hs] = hb[:, 2 * W_B:3 * W_B]
            y = hb[:, 0:W_B] * jax.nn.sigmoid(hb[:, W_B:2 * W_B])
            y_r[bs] = jnp.where(p_new < n_groups_seq, y, 0.0)

        def piece_q():
            q_r[hs] = (_dot(xb_s[...], win_ref[:, COL_Q:COL_KV]) * Q_SCALE).astype(_BF16)

        def piece_kv():
            kv = _dot(xb_s[...], win_ref[:, COL_KV:COL_CG])
            zero = jnp.zeros((GROUP_ROWS, LANES), _BF16)
            for t in range(2):
                nat = kv[:, t * KV_W:(t + 1) * KV_W]
                swp = pltpu.roll(nat, HEAD_DIM, axis=1).astype(_BF16)
                nat = nat.astype(_BF16)
                kvar_r[bs, 4 * t + 0] = jnp.where(low, nat, zero)
                kvar_r[bs, 4 * t + 1] = jnp.where(low, zero, swp)
                kvar_r[bs, 4 * t + 2] = jnp.where(low, swp, zero)
                kvar_r[bs, 4 * t + 3] = jnp.where(low, zero, nat)

        def piece_cg():
            cg_r[hs] = _dot(xb_s[...], win_ref[:, COL_CG:D_IN])

        return [piece_kv, piece_b, piece_a, piece_q, piece_cg]

    def mix_block(b, p, issue):
        hs = p & (H_SLOTS - 1)
        s_here = p & (BAND_SLOTS - 1)
        blk = [slice(k * BLOCK, (k + 1) * BLOCK) for k in range(GROUP_BLOCKS)]
        here = blk[b]
        before = ((p + BAND_SLOTS - 1) & (BAND_SLOTS - 1), blk[-1]) if b == 0 else (s_here, blk[b - 1])
        after = ((p + 1) & (BAND_SLOTS - 1), blk[0]) if b == GROUP_BLOCKS - 1 else (s_here, blk[b + 1])
        band_at = [before, (s_here, here), after]
        g = GROUP_BLOCKS * p + b

        ha = ha_r[hs, here, :]
        u = jax.nn.gelu(ha[:, 0:W_A])
        v = _layer_norm(jax.nn.gelu(ha[:, W_A:2 * W_A]), gg_ref[...], gb_ref[...])
        gate_a = jax.nn.silu(ha[:, 2 * W_A:3 * W_A])
        vb = v.astype(_BF16)
        zero = jnp.zeros_like(vb)
        rhs = jnp.concatenate([jnp.where(head_of_lane == h, vb, zero)
                               for h in range(N_HEADS_A)], axis=0)
        sp = _dot(wcat_ref[...], rhs) + bsp_ref[...]
        ya = (u * sp * gate_a).astype(_BF16)

        (sp_, rp_), (sc_, rc_), (sn_, rn_) = band_at
        ywin = jnp.concatenate([y_r[sp_, rp_.stop - CONV_HALO:rp_.stop, :], y_r[sc_, rc_, :],
                                y_r[sn_, rn_.start:rn_.start + CONV_HALO, :]], axis=0)
        bg = bg_r[hs, here, :]
        span = CONV_ROWS + 2 * CONV_HALO
        yb_parts = []
        for ci in range(BLOCK // CONV_ROWS):
            issue()
            base = ci * CONV_ROWS
            ych = ywin[base:base + span, :]
            acc = jnp.zeros((CONV_ROWS, W_B), _F32)
            for r in range(SUBLANES):
                zr = ych if r == 0 else pltpu.roll(ych, span - r, axis=0)
                for m in range(2 * CONV_HALO // SUBLANES):
                    k = SUBLANES * m + r - 1
                    if 0 <= k < CONV_WIDTH:
                        acc = acc + zr[SUBLANES * m:SUBLANES * m + CONV_ROWS, :] * cw_ref[k:k + 1, :]
            z = _layer_norm(acc + cb_ref[...], clg_ref[...], clb_ref[...])
            gate_b = jax.nn.silu(bg[base:base + CONV_ROWS, :])
            yb_parts.append((jax.nn.silu(z) * gate_b).astype(_BF16))
        yb = jnp.concatenate(yb_parts, axis=0)

        edge = jnp.where(g == 0, 1, jnp.where(g == nblk_seq - 1, 2, 0))
        q_blk = q_r[hs, here, :]
        cg = cg_r[hs, here, :]

        def band(var):
            return [kvar_r[s_, var, r_, :] for (s_, r_) in band_at]

        yc_parts = []
        for kvh in range(N_KV_HEADS):
            issue()
            qg = q_blk[:, kvh * 2 * LANES:(kvh + 1) * 2 * LANES]
            lhs = jnp.concatenate([qg[:, 0:LANES], qg[:, LANES:2 * LANES]], axis=0)
            kcat = jnp.concatenate(band(2 * kvh) + band(2 * kvh + 1), axis=0)
            s2 = lax.dot_general(lhs, kcat, (((1,), (1,)), ((), ())),
                                 preferred_element_type=_F32)
            p_rows = []
            sink_w = []
            for pp in range(2):
                p_pair = []
                for hh in range(2):
                    h = 4 * kvh + 2 * pp + hh
                    s = (s2[pp * BLOCK:(pp + 1) * BLOCK, hh * 3 * BLOCK:(hh + 1) * 3 * BLOCK]
                         + biasm_ref[edge, h])
                    sk = sink_ref[h] * LOG2_E
                    m = jnp.maximum(jnp.max(s, axis=-1, keepdims=True), sk)
                    p_pair.append(jnp.exp2((s - m).astype(_BF16)))
                    sink_w.append(jnp.exp2(sk - m))
                p_rows.append(jnp.concatenate(p_pair, axis=1))
            pcat = jnp.concatenate(p_rows, axis=0)
            vcat = jnp.concatenate(
                [jnp.concatenate(band(4 + 2 * kvh) + band(5 + 2 * kvh), axis=0), ones_ref[...]], axis=1)
            o2 = _dot(pcat, vcat)
            for pp in range(2):
                rsl = slice(pp * BLOCK, (pp + 1) * BLOCK)
                cols = slice((2 * kvh + pp) * LANES, (2 * kvh + pp + 1) * LANES)
                den = o2[rsl, LANES:2 * LANES] + jnp.where(low, sink_w[2 * pp], sink_w[2 * pp + 1])
                o = o2[rsl, 0:LANES] / den
                yc_parts.append((o * jax.nn.silu(cg[:, cols])).astype(_BF16))
        return jnp.concatenate([ya, yb] + yc_parts, axis=1)

    def mix(jg, p, pieces):
        pieces = list(pieces)

        def issue(n=1):
            for _ in range(n):
                if pieces:
                    pieces.pop(0)()
        rows = pl.ds(pl.multiple_of(jg * GROUP_ROWS, GROUP_ROWS), GROUP_ROWS)
        hs = p & (H_SLOTS - 1)
        issue(2)
        ymix = jnp.concatenate([mix_block(b, p, issue) for b in range(GROUP_BLOCKS)], axis=0)

        issue(len(pieces))
        y_out = _dot(ymix, wout_ref[...])
        x_res = xres_r[hs] if first_layer else xc_ref[rows, :]
        out_ref[rows, :] = _layer_norm(DEEPNORM_ALPHA * x_res + y_out, pg_ref[...], pb_ref[...])

    p0 = i * n_groups

    @pl.when(i == 0)
    def _():
        kvar_r[BAND_SLOTS - 1] = jnp.zeros(kvar_r.shape[1:], _BF16)
        y_r[BAND_SLOTS - 1] = jnp.zeros(y_r.shape[1:], _F32)
        for piece in project_pieces(xc_ref[0:GROUP_ROWS, :], p0):
            piece()

    def body(jg, carry):
        p = p0 + jg
        if n_groups == 1:
            x_next = xn_ref[...]
        else:
            nxt = pl.multiple_of(jnp.minimum(jg + 1, n_groups - 1) * GROUP_ROWS, GROUP_ROWS)
            x_next = jnp.where(jg + 1 < n_groups, xc_ref[pl.ds(nxt, GROUP_ROWS), :], xn_ref[...])
        mix(jg, p, project_pieces(x_next, p + 1))
        return carry

    lax.fori_loop(0, n_groups, body, 0)


def _t5_bucket(rel):
    nb = N_BUCKETS // 2
    max_exact = nb // 2
    ret = jnp.where(rel > 0, nb, 0)
    n = jnp.abs(rel)
    nf = jnp.maximum(n, 1).astype(jnp.float32)
    large = max_exact + (jnp.log(nf / max_exact) / math.log(MAX_DISTANCE / max_exact)
                         * (nb - max_exact)).astype(jnp.int32)
    large = jnp.minimum(large, nb - 1)
    return ret + jnp.where(n < max_exact, n, large)


def _const_spec(shape):
    zeros = (0,) * len(shape)
    return pl.BlockSpec(shape, lambda b, i: zeros, pipeline_mode=pl.Buffered(1))


def _layer_slice_spec(stacked, layer):
    return pl.BlockSpec((None,) + stacked.shape[1:], lambda b, i: (layer, 0, 0),
                        pipeline_mode=pl.Buffered(1))


def _layer_call(layer, x, ling, linb, win, gg, gb, wcat, bsp, cw, cb, clg, clb,
                sink, biasm, ones, wout, pg, pb):
    first_layer = layer == 0
    bsz, seq, d = x.shape
    tm = TILE_ROWS
    assert seq % tm == 0 and tm % GROUP_ROWS == 0 and seq // GROUP_ROWS >= 2
    groups_per_tile = tm // GROUP_ROWS
    nblk_seq = seq // BLOCK
    grid = (bsz, seq // tm)

    in_specs = [
        pl.BlockSpec((None, tm, d), lambda b, i: (b, i, 0)),
        pl.BlockSpec((None, GROUP_ROWS, d),
                     lambda b, i: (b, jnp.minimum((i + 1) * groups_per_tile,
                                                  seq // GROUP_ROWS - 1), 0)),
        _const_spec(ling.shape), _const_spec(linb.shape), _layer_slice_spec(win, layer),
        _const_spec(gg.shape), _const_spec(gb.shape), _const_spec(wcat.shape),
        _const_spec(bsp.shape),
        _const_spec(cw.shape), _const_spec(cb.shape), _const_spec(clg.shape),
        _const_spec(clb.shape),
        pl.BlockSpec(memory_space=pltpu.SMEM),
        _const_spec(biasm.shape), _const_spec(ones.shape), _layer_slice_spec(wout, layer),
        _const_spec(pg.shape), _const_spec(pb.shape),
    ]
    scratch = [
        pltpu.VMEM((GROUP_ROWS, d), _BF16),
        pltpu.VMEM((H_SLOTS, GROUP_ROWS if first_layer else SUBLANES, d), _F32),
        pltpu.VMEM((H_SLOTS, GROUP_ROWS, 3 * W_A), _F32),
        pltpu.VMEM((H_SLOTS, GROUP_ROWS, W_B), _F32),
        pltpu.VMEM((H_SLOTS, GROUP_ROWS, W_C), _BF16),
        pltpu.VMEM((H_SLOTS, GROUP_ROWS, W_C), _F32),
        pltpu.VMEM((BAND_SLOTS, 8, GROUP_ROWS, LANES), _BF16),
        pltpu.VMEM((BAND_SLOTS, GROUP_ROWS, W_B), _F32),
    ]
    return pl.pallas_call(
        functools.partial(_layer_kernel, first_layer, tm, nblk_seq),
        grid=grid,
        in_specs=in_specs,
        out_specs=pl.BlockSpec((None, tm, d), lambda b, i: (b, i, 0)),
        out_shape=jax.ShapeDtypeStruct(x.shape, x.dtype),
        scratch_shapes=scratch,
        compiler_params=pltpu.CompilerParams(
            dimension_semantics=("arbitrary", "arbitrary"),
            vmem_limit_bytes=VMEM_LIMIT_BYTES),
        name="layer_first" if first_layer else "layer_next",
    )(x, x, ling, linb, win, gg, gb, wcat, bsp, cw, cb, clg, clb, sink, biasm, ones, wout, pg, pb)


def kernel(x, ln_in_g, ln_in_b, w_in, gmlp_ln_g, gmlp_ln_b, w_spatial, b_spatial, conv_w, conv_b,
           conv_ln_g, conv_ln_b, attn_sink, rel_bias, w_out, post_ln_g, post_ln_b):
    depth = w_in.shape[0]
    assert depth == DEPTH and x.shape[2] == D_MODEL and w_in.shape[2] == D_IN

    qq = jnp.arange(BLOCK)[:, None]
    kk = jnp.arange(3 * BLOCK)[None, :]
    bucket = _t5_bucket(kk - BLOCK - qq).astype(jnp.int32)
    biasm = pl.pallas_call(
        _bias_kernel,
        in_specs=[pl.BlockSpec(memory_space=pltpu.VMEM), pl.BlockSpec(memory_space=pltpu.SMEM)],
        out_specs=pl.BlockSpec(memory_space=pltpu.VMEM),
        out_shape=jax.ShapeDtypeStruct((3, N_Q_HEADS, BLOCK, 3 * BLOCK), _F32),
        name="rel_bias_tables",
    )(bucket, rel_bias.astype(_F32))

    kidx = jnp.arange(6 * BLOCK)[:, None] // (3 * BLOCK)
    ones = (kidx == jnp.arange(LANES)[None, :] // HEAD_DIM).astype(_BF16)

    row = lambda a: a.reshape(1, -1).astype(_F32)
    win_all = w_in.astype(_BF16)
    wout_all = w_out.astype(_BF16)
    for l in range(depth):
        wcat = jnp.transpose(w_spatial[l], (1, 0, 2)).reshape(BLOCK, N_HEADS_A * BLOCK).astype(_BF16)
        bsp = jnp.repeat(b_spatial[l].T, HEAD_DIM, axis=1).astype(_F32)
        x = _layer_call(
            l, x, row(ln_in_g), row(ln_in_b), win_all,
            row(gmlp_ln_g[l]), row(gmlp_ln_b[l]), wcat, bsp,
            conv_w[l].astype(_F32), row(conv_b[l]), row(conv_ln_g[l]), row(conv_ln_b[l]),
            attn_sink[l].astype(_F32), biasm, ones, wout_all,
            row(post_ln_g[l]), row(post_ln_b[l]))
    return x
```

```python
import functools
import math

import jax
import jax.numpy as jnp
from jax import lax
from jax.experimental import pallas as pl
from jax.experimental.pallas import tpu as pltpu

D_MODEL = 1024
HEAD_DIM = 64
W_A = 256
W_B = 256
W_C = 512
N_HEADS_A = 4
N_Q_HEADS = 8
N_KV_HEADS = 2
KV_W = N_KV_HEADS * HEAD_DIM
CONV_WIDTH = 31
CONV_PAD = CONV_WIDTH // 2
BLOCK = 128
N_BUCKETS = 32
MAX_DISTANCE = 128
LN_EPS = 1e-5
NEG_INF = -1e30
DEPTH = 2
DEEPNORM_ALPHA = (2 * DEPTH) ** 0.25
LOG2_E = math.log2(math.e)
Q_SCALE = HEAD_DIM ** -0.5 * LOG2_E

COL_A = 0
COL_B = 3 * W_A
COL_Q = COL_B + 3 * W_B
COL_KV = COL_Q + W_C
COL_CG = COL_KV + 2 * KV_W
D_IN = COL_CG + W_C

LANES = 128
SUBLANES = 8
CONV_HALO = 16
CONV_ROWS = 64
GROUP_BLOCKS = 4
GROUP_ROWS = GROUP_BLOCKS * BLOCK
H_SLOTS = 2
BAND_SLOTS = 4
TILE_ROWS = GROUP_ROWS
VMEM_LIMIT_BYTES = 60 * 1024 * 1024

_BF16 = jnp.bfloat16
_F32 = jnp.float32


def _layer_norm(x, g, b):
    mu = jnp.mean(x, axis=-1, keepdims=True)
    xc = x - mu
    var = jnp.mean(xc * xc, axis=-1, keepdims=True)
    return xc * lax.rsqrt(var + LN_EPS) * g + b


def _dot(a, b):
    return jnp.dot(a, b, preferred_element_type=_F32)


def _bias_kernel(bucket_ref, rb_ref, out_ref):
    bucket = bucket_ref[...]
    row = lax.broadcasted_iota(jnp.int32, (BLOCK, 3 * BLOCK), 0)
    col = lax.broadcasted_iota(jnp.int32, (BLOCK, 3 * BLOCK), 1)
    in_window = jnp.abs(col - BLOCK - row) <= BLOCK
    for h in range(N_Q_HEADS):
        acc = jnp.zeros((BLOCK, 3 * BLOCK), _F32)
        for b in range(N_BUCKETS):
            acc = jnp.where(bucket == b, rb_ref[b, h] * LOG2_E, acc)
        base = jnp.where(in_window, acc, NEG_INF)
        out_ref[0, h] = base
        out_ref[1, h] = jnp.where(col < BLOCK, NEG_INF, base)
        out_ref[2, h] = jnp.where(col >= 2 * BLOCK, NEG_INF, base)


def _layer_kernel(first_layer, tm, nblk_seq,
                  xc_ref, xn_ref, ling_ref, linb_ref, win_ref,
                  gg_ref, gb_ref, wcat_ref, bsp_ref,
                  cw_ref, cb_ref, clg_ref, clb_ref,
                  sink_ref, biasm_ref, ones_ref, wout_ref, pg_ref, pb_ref,
                  out_ref,
                  xb_s, xres_r, ha_r, bg_r, q_r, cg_r, kvar_r, y_r):
    i = pl.program_id(1)
    n_groups = tm // GROUP_ROWS
    n_groups_seq = nblk_seq // GROUP_BLOCKS
    low = lax.broadcasted_iota(jnp.int32, (1, LANES), 1) < HEAD_DIM
    head_of_lane = lax.broadcasted_iota(jnp.int32, (1, W_A), 1) // HEAD_DIM

    def project_pieces(x_grp, p_new):
        hs = p_new & (H_SLOTS - 1)
        bs = p_new & (BAND_SLOTS - 1)
        if first_layer:
            x_grp = _layer_norm(x_grp, ling_ref[...], linb_ref[...])
            xres_r[hs] = x_grp
        xb_s[...] = x_grp.astype(_BF16)

        def piece_a():
            ha_r[hs] = _dot(xb_s[...], win_ref[:, COL_A:COL_B])

        def piece_b():
            hb = _dot(xb_s[...], win_ref[:, COL_B:COL_Q])
            bg_r[hs] = hb[:, 2 * W_B:3 * W_B]
            y = hb[:, 0:W_B] * jax.nn.sigmoid(hb[:, W_B:2 * W_B])
            y_r[bs] = jnp.where(p_new < n_groups_seq, y, 0.0)

        def piece_q():
            q_r[hs] = (_dot(xb_s[...], win_ref[:, COL_Q:COL_KV]) * Q_SCALE).astype(_BF16)

        def piece_kv():
            kv = _dot(xb_s[...], win_ref[:, COL_KV:COL_CG])
            zero = jnp.zeros((GROUP_ROWS, LANES), _BF16)
            for t in range(2):
                nat = kv[:, t * KV_W:(t + 1) * KV_W]
                swp = pltpu.roll(nat, HEAD_DIM, axis=1).astype(_BF16)
                nat = nat.astype(_BF16)
                kvar_r[bs, 4 * t + 0] = jnp.where(low, nat, zero)
                kvar_r[bs, 4 * t + 1] = jnp.where(low, zero, swp)
                kvar_r[bs, 4 * t + 2] = jnp.where(low, swp, zero)
                kvar_r[bs, 4 * t + 3] = jnp.where(low, zero, nat)

        def piece_cg():
            cg_r[hs] = _dot(xb_s[...], win_ref[:, COL_CG:D_IN])

        return [piece_kv, piece_b, piece_q, piece_cg, piece_a]

    def mix_block(b, p, issue):
        hs = p & (H_SLOTS - 1)
        s_here = p & (BAND_SLOTS - 1)
        blk = [slice(k * BLOCK, (k + 1) * BLOCK) for k in range(GROUP_BLOCKS)]
        here = blk[b]
        before = ((p + BAND_SLOTS - 1) & (BAND_SLOTS - 1), blk[-1]) if b == 0 else (s_here, blk[b - 1])
        after = ((p + 1) & (BAND_SLOTS - 1), blk[0]) if b == GROUP_BLOCKS - 1 else (s_here, blk[b + 1])
        band_at = [before, (s_here, here), after]
        g = GROUP_BLOCKS * p + b

        ha = ha_r[hs, here, :]
        u = jax.nn.gelu(ha[:, 0:W_A])
        v = _layer_norm(jax.nn.gelu(ha[:, W_A:2 * W_A]), gg_ref[...], gb_ref[...])
        gate_a = jax.nn.silu(ha[:, 2 * W_A:3 * W_A])
        vb = v.astype(_BF16)
        zero = jnp.zeros_like(vb)
        rhs = jnp.concatenate([jnp.where(head_of_lane == h, vb, zero)
                               for h in range(N_HEADS_A)], axis=0)
        sp = _dot(wcat_ref[...], rhs) + bsp_ref[...]
        ya = (u * sp * gate_a).astype(_BF16)

        (sp_, rp_), (sc_, rc_), (sn_, rn_) = band_at
        ywin = jnp.concatenate([y_r[sp_, rp_.stop - CONV_HALO:rp_.stop, :], y_r[sc_, rc_, :],
                                y_r[sn_, rn_.start:rn_.start + CONV_HALO, :]], axis=0)
        bg = bg_r[hs, here, :]
        span = CONV_ROWS + 2 * CONV_HALO
        yb_parts = []
        for ci in range(BLOCK // CONV_ROWS):
            issue()
            base = ci * CONV_ROWS
            ych = ywin[base:base + span, :]
            acc = jnp.zeros((CONV_ROWS, W_B), _F32)
            for r in range(SUBLANES):
                zr = ych if r == 0 else pltpu.roll(ych, span - r, axis=0)
                for m in range(2 * CONV_HALO // SUBLANES):
                    k = SUBLANES * m + r - 1
                    if 0 <= k < CONV_WIDTH:
                        acc = acc + zr[SUBLANES * m:SUBLANES * m + CONV_ROWS, :] * cw_ref[k:k + 1, :]
            z = _layer_norm(acc + cb_ref[...], clg_ref[...], clb_ref[...])
            gate_b = jax.nn.silu(bg[base:base + CONV_ROWS, :])
            yb_parts.append((jax.nn.silu(z) * gate_b).astype(_BF16))
        yb = jnp.concatenate(yb_parts, axis=0)

        edge = jnp.where(g == 0, 1, jnp.where(g == nblk_seq - 1, 2, 0))
        q_blk = q_r[hs, here, :]
        cg = cg_r[hs, here, :]

        def band(var):
            return [kvar_r[s_, var, r_, :] for (s_, r_) in band_at]

        yc_parts = []
        for kvh in range(N_KV_HEADS):
            issue()
            qg = q_blk[:, kvh * 2 * LANES:(kvh + 1) * 2 * LANES]
            lhs = jnp.concatenate([qg[:, 0:LANES], qg[:, LANES:2 * LANES]], axis=0)
            kcat = jnp.concatenate(band(2 * kvh) + band(2 * kvh + 1), axis=0)
            s2 = lax.dot_general(lhs, kcat, (((1,), (1,)), ((), ())),
                                 preferred_element_type=_F32)
            p_rows = []
            sink_w = []
            for pp in range(2):
                p_pair = []
                for hh in range(2):
                    h = 4 * kvh + 2 * pp + hh
                    s = (s2[pp * BLOCK:(pp + 1) * BLOCK, hh * 3 * BLOCK:(hh + 1) * 3 * BLOCK]
                         + biasm_ref[edge, h])
                    sk = sink_ref[h] * LOG2_E
                    m = jnp.maximum(jnp.max(s, axis=-1, keepdims=True), sk)
                    p_pair.append(jnp.exp2((s - m).astype(_BF16)))
                    sink_w.append(jnp.exp2(sk - m))
                p_rows.append(jnp.concatenate(p_pair, axis=1))
            pcat = jnp.concatenate(p_rows, axis=0)
            vcat = jnp.concatenate(
                [jnp.concatenate(band(4 + 2 * kvh) + band(5 + 2 * kvh), axis=0), ones_ref[...]], axis=1)
            o2 = _dot(pcat, vcat)
            for pp in range(2):
                rsl = slice(pp * BLOCK, (pp + 1) * BLOCK)
                cols = slice((2 * kvh + pp) * LANES, (2 * kvh + pp + 1) * LANES)
                den = o2[rsl, LANES:2 * LANES] + jnp.where(low, sink_w[2 * pp], sink_w[2 * pp + 1])
                o = o2[rsl, 0:LANES] / den
                yc_parts.append((o * jax.nn.silu(cg[:, cols])).astype(_BF16))
        return jnp.concatenate([ya, yb] + yc_parts, axis=1)

    def mix(jg, p, pieces):
        pieces = list(pieces)

        def issue(n=1):
            for _ in range(n):
                if pieces:
                    pieces.pop(0)()
        rows = pl.ds(pl.multiple_of(jg * GROUP_ROWS, GROUP_ROWS), GROUP_ROWS)
        hs = p & (H_SLOTS - 1)
        issue(2)
        ymix = jnp.concatenate([mix_block(b, p, issue) for b in range(GROUP_BLOCKS)], axis=0)

        issue(len(pieces))
        y_out = _dot(ymix, wout_ref[...])
        x_res = xres_r[hs] if first_layer else xc_ref[rows, :]
        out_ref[rows, :] = _layer_norm(DEEPNORM_ALPHA * x_res + y_out, pg_ref[...], pb_ref[...])

    p0 = i * n_groups

    @pl.when(i == 0)
    def _():
        kvar_r[BAND_SLOTS - 1] = jnp.zeros(kvar_r.shape[1:], _BF16)
        y_r[BAND_SLOTS - 1] = jnp.zeros(y_r.shape[1:], _F32)
        for piece in project_pieces(xc_ref[0:GROUP_ROWS, :], p0):
            piece()

    def body(jg, carry):
        p = p0 + jg
        if n_groups == 1:
            x_next = xn_ref[...]
        else:
            nxt = pl.multiple_of(jnp.minimum(jg + 1, n_groups - 1) * GROUP_ROWS, GROUP_ROWS)
            x_next = jnp.where(jg + 1 < n_groups, xc_ref[pl.ds(nxt, GROUP_ROWS), :], xn_ref[...])
        mix(jg, p, project_pieces(x_next, p + 1))
        return carry

    lax.fori_loop(0, n_groups, body, 0)


def _t5_bucket(rel):
    nb = N_BUCKETS // 2
    max_exact = nb // 2
    ret = jnp.where(rel > 0, nb, 0)
    n = jnp.abs(rel)
    nf = jnp.maximum(n, 1).astype(jnp.float32)
    large = max_exact + (jnp.log(nf / max_exact) / math.log(MAX_DISTANCE / max_exact)
                         * (nb - max_exact)).astype(jnp.int32)
    large = jnp.minimum(large, nb - 1)
    return ret + jnp.where(n < max_exact, n, large)


def _const_spec(shape):
    zeros = (0,) * len(shape)
    return pl.BlockSpec(shape, lambda b, i: zeros, pipeline_mode=pl.Buffered(1))


def _layer_slice_spec(stacked, layer):
    return pl.BlockSpec((None,) + stacked.shape[1:], lambda b, i: (layer, 0, 0),
                        pipeline_mode=pl.Buffered(1))


def _layer_call(layer, x, ling, linb, win, gg, gb, wcat, bsp, cw, cb, clg, clb,
                sink, biasm, ones, wout, pg, pb):
    first_layer = layer == 0
    bsz, seq, d = x.shape
    tm = TILE_ROWS
    assert seq % tm == 0 and tm % GROUP_ROWS == 0 and seq // GROUP_ROWS >= 2
    groups_per_tile = tm // GROUP_ROWS
    nblk_seq = seq // BLOCK
    grid = (bsz, seq // tm)

    in_specs = [
        pl.BlockSpec((None, tm, d), lambda b, i: (b, i, 0)),
        pl.BlockSpec((None, GROUP_ROWS, d),
                     lambda b, i: (b, jnp.minimum((i + 1) * groups_per_tile,
                                                  seq // GROUP_ROWS - 1), 0)),
        _const_spec(ling.shape), _const_spec(linb.shape), _layer_slice_spec(win, layer),
        _const_spec(gg.shape), _const_spec(gb.shape), _const_spec(wcat.shape),
        _const_spec(bsp.shape),
        _const_spec(cw.shape), _const_spec(cb.shape), _const_spec(clg.shape),
        _const_spec(clb.shape),
        pl.BlockSpec(memory_space=pltpu.SMEM),
        _const_spec(biasm.shape), _const_spec(ones.shape), _layer_slice_spec(wout, layer),
        _const_spec(pg.shape), _const_spec(pb.shape),
    ]
    scratch = [
        pltpu.VMEM((GROUP_ROWS, d), _BF16),
        pltpu.VMEM((H_SLOTS, GROUP_ROWS if first_layer else SUBLANES, d), _F32),
        pltpu.VMEM((H_SLOTS, GROUP_ROWS, 3 * W_A), _F32),
        pltpu.VMEM((H_SLOTS, GROUP_ROWS, W_B), _F32),
        pltpu.VMEM((H_SLOTS, GROUP_ROWS, W_C), _BF16),
        pltpu.VMEM((H_SLOTS, GROUP_ROWS, W_C), _F32),
        pltpu.VMEM((BAND_SLOTS, 8, GROUP_ROWS, LANES), _BF16),
        pltpu.VMEM((BAND_SLOTS, GROUP_ROWS, W_B), _F32),
    ]
    return pl.pallas_call(
        functools.partial(_layer_kernel, first_layer, tm, nblk_seq),
        grid=grid,
        in_specs=in_specs,
        out_specs=pl.BlockSpec((None, tm, d), lambda b, i: (b, i, 0)),
        out_shape=jax.ShapeDtypeStruct(x.shape, x.dtype),
        scratch_shapes=scratch,
        compiler_params=pltpu.CompilerParams(
            dimension_semantics=("arbitrary", "arbitrary"),
            vmem_limit_bytes=VMEM_LIMIT_BYTES),
        name="layer_first" if first_layer else "layer_next",
    )(x, x, ling, linb, win, gg, gb, wcat, bsp, cw, cb, clg, clb, sink, biasm, ones, wout, pg, pb)


def kernel(x, ln_in_g, ln_in_b, w_in, gmlp_ln_g, gmlp_ln_b, w_spatial, b_spatial, conv_w, conv_b,
           conv_ln_g, conv_ln_b, attn_sink, rel_bias, w_out, post_ln_g, post_ln_b):
    depth = w_in.shape[0]
    assert depth == DEPTH and x.shape[2] == D_MODEL and w_in.shape[2] == D_IN

    qq = jnp.arange(BLOCK)[:, None]
    kk = jnp.arange(3 * BLOCK)[None, :]
    bucket = _t5_bucket(kk - BLOCK - qq).astype(jnp.int32)
    biasm = pl.pallas_call(
        _bias_kernel,
        in_specs=[pl.BlockSpec(memory_space=pltpu.VMEM), pl.BlockSpec(memory_space=pltpu.SMEM)],
        out_specs=pl.BlockSpec(memory_space=pltpu.VMEM),
        out_shape=jax.ShapeDtypeStruct((3, N_Q_HEADS, BLOCK, 3 * BLOCK), _F32),
        name="rel_bias_tables",
    )(bucket, rel_bias.astype(_F32))

    kidx = jnp.arange(6 * BLOCK)[:, None] // (3 * BLOCK)
    ones = (kidx == jnp.arange(LANES)[None, :] // HEAD_DIM).astype(_BF16)

    row = lambda a: a.reshape(1, -1).astype(_F32)
    win_all = w_in.astype(_BF16)
    wout_all = w_out.astype(_BF16)
    for l in range(depth):
        wcat = jnp.transpose(w_spatial[l], (1, 0, 2)).reshape(BLOCK, N_HEADS_A * BLOCK).astype(_BF16)
        bsp = jnp.repeat(b_spatial[l].T, HEAD_DIM, axis=1).astype(_F32)
        x = _layer_call(
            l, x, row(ln_in_g), row(ln_in_b), win_all,
            row(gmlp_ln_g[l]), row(gmlp_ln_b[l]), wcat, bsp,
            conv_w[l].astype(_F32), row(conv_b[l]), row(conv_ln_g[l]), row(conv_ln_b[l]),
            attn_sink[l].astype(_F32), biasm, ones, wout_all,
            row(post_ln_g[l]), row(post_ln_b[l]))
    return x
```

```python
import functools
import math

import jax
import jax.numpy as jnp
from jax import lax
from jax.experimental import pallas as pl
from jax.experimental.pallas import tpu as pltpu

D_MODEL = 1024
HEAD_DIM = 64
W_A = 256
W_B = 256
W_C = 512
N_HEADS_A = 4
N_Q_HEADS = 8
N_KV_HEADS = 2
KV_W = N_KV_HEADS * HEAD_DIM
CONV_WIDTH = 31
CONV_PAD = CONV_WIDTH // 2
BLOCK = 128
N_BUCKETS = 32
MAX_DISTANCE = 128
LN_EPS = 1e-5
NEG_INF = -1e30
DEPTH = 2
DEEPNORM_ALPHA = (2 * DEPTH) ** 0.25
LOG2_E = math.log2(math.e)
Q_SCALE = HEAD_DIM ** -0.5 * LOG2_E

COL_A = 0
COL_B = 3 * W_A
COL_Q = COL_B + 3 * W_B
COL_KV = COL_Q + W_C
COL_CG = COL_KV + 2 * KV_W
D_IN = COL_CG + W_C

LANES = 128
SUBLANES = 8
CONV_HALO = 16
CONV_ROWS = 64
GROUP_BLOCKS = 4
GROUP_ROWS = GROUP_BLOCKS * BLOCK
H_SLOTS = 2
BAND_SLOTS = 4
TILE_ROWS = GROUP_ROWS
VMEM_LIMIT_BYTES = 60 * 1024 * 1024

(V_LN_IN_G, V_LN_IN_B, V_POST_LN_G, V_POST_LN_B, V_GMLP_LN_G, V_GMLP_LN_B,
 V_CONV_B, V_CONV_LN_G, V_CONV_LN_B) = range(9)
VEC_ROWS = 16

_BF16 = jnp.bfloat16
_F32 = jnp.float32


def _layer_norm(x, g, b):
    mu = jnp.mean(x, axis=-1, keepdims=True)
    xc = x - mu
    var = jnp.mean(xc * xc, axis=-1, keepdims=True)
    return xc * lax.rsqrt(var + LN_EPS) * g + b


def _dot(a, b):
    return jnp.dot(a, b, preferred_element_type=_F32)


def _bias_kernel(bucket_ref, rb_ref, out_ref):
    bucket = bucket_ref[...]
    row = lax.broadcasted_iota(jnp.int32, (BLOCK, 3 * BLOCK), 0)
    col = lax.broadcasted_iota(jnp.int32, (BLOCK, 3 * BLOCK), 1)
    in_window = jnp.abs(col - BLOCK - row) <= BLOCK
    for h in range(N_Q_HEADS):
        acc = jnp.zeros((BLOCK, 3 * BLOCK), _F32)
        for b in range(N_BUCKETS):
            acc = jnp.where(bucket == b, rb_ref[b, h] * LOG2_E, acc)
        base = jnp.where(in_window, acc, NEG_INF)
        out_ref[0, h] = base
        out_ref[1, h] = jnp.where(col < BLOCK, NEG_INF, base)
        out_ref[2, h] = jnp.where(col >= 2 * BLOCK, NEG_INF, base)


def _layer_kernel(layer, tm, nblk_seq,
                  xc_ref, xn_ref, vec_ref, win_ref, wcat_ref, bsp_ref, cw_ref,
                  sink_ref, biasm_ref, ones_ref, wout_ref,
                  out_ref,
                  xb_s, xres_r, ha_r, bg_r, q_r, cg_r, kvar_r, y_r):
    first_layer = layer == 0
    i = pl.program_id(1)
    n_groups = tm // GROUP_ROWS
    n_groups_seq = nblk_seq // GROUP_BLOCKS
    low = lax.broadcasted_iota(jnp.int32, (1, LANES), 1) < HEAD_DIM
    head_of_lane = lax.broadcasted_iota(jnp.int32, (1, W_A), 1) // HEAD_DIM

    def vec(row, width=D_MODEL):
        return vec_ref[row:row + 1, 0:width]

    def project_pieces(x_grp, p_new):
        hs = p_new & (H_SLOTS - 1)
        bs = p_new & (BAND_SLOTS - 1)
        if first_layer:
            x_grp = _layer_norm(x_grp, vec(V_LN_IN_G), vec(V_LN_IN_B))
            xres_r[hs] = x_grp
        xb_s[...] = x_grp.astype(_BF16)

        def piece_a():
            ha_r[hs] = _dot(xb_s[...], win_ref[:, COL_A:COL_B])

        def piece_b():
            hb = _dot(xb_s[...], win_ref[:, COL_B:COL_Q])
            bg_r[hs] = hb[:, 2 * W_B:3 * W_B]
            y = hb[:, 0:W_B] * jax.nn.sigmoid(hb[:, W_B:2 * W_B])
            y_r[bs] = jnp.where(p_new < n_groups_seq, y, 0.0)

        def piece_q():
            q_r[hs] = (_dot(xb_s[...], win_ref[:, COL_Q:COL_KV]) * Q_SCALE).astype(_BF16)

        def piece_kv():
            kv = _dot(xb_s[...], win_ref[:, COL_KV:COL_CG])
            zero = jnp.zeros((GROUP_ROWS, LANES), _BF16)
            for t in range(2):
                nat = kv[:, t * KV_W:(t + 1) * KV_W]
                swp = pltpu.roll(nat, HEAD_DIM, axis=1).astype(_BF16)
                nat = nat.astype(_BF16)
                kvar_r[bs, 4 * t + 0] = jnp.where(low, nat, zero)
                kvar_r[bs, 4 * t + 1] = jnp.where(low, zero, swp)
                kvar_r[bs, 4 * t + 2] = jnp.where(low, swp, zero)
                kvar_r[bs, 4 * t + 3] = jnp.where(low, zero, nat)

        def piece_cg():
            cg_r[hs] = _dot(xb_s[...], win_ref[:, COL_CG:D_IN])

        return [piece_kv, piece_b, piece_q, piece_cg, piece_a]

    def mix_block(b, p, issue):
        hs = p & (H_SLOTS - 1)
        s_here = p & (BAND_SLOTS - 1)
        blk = [slice(k * BLOCK, (k + 1) * BLOCK) for k in range(GROUP_BLOCKS)]
        here = blk[b]
        before = ((p + BAND_SLOTS - 1) & (BAND_SLOTS - 1), blk[-1]) if b == 0 else (s_here, blk[b - 1])
        after = ((p + 1) & (BAND_SLOTS - 1), blk[0]) if b == GROUP_BLOCKS - 1 else (s_here, blk[b + 1])
        band_at = [before, (s_here, here), after]
        g = GROUP_BLOCKS * p + b

        ha = ha_r[hs, here, :]
        u = jax.nn.gelu(ha[:, 0:W_A])
        v = _layer_norm(jax.nn.gelu(ha[:, W_A:2 * W_A]), vec(V_GMLP_LN_G, W_A), vec(V_GMLP_LN_B, W_A))
        gate_a = jax.nn.silu(ha[:, 2 * W_A:3 * W_A])
        vb = v.astype(_BF16)
        zero = jnp.zeros_like(vb)
        rhs = jnp.concatenate([jnp.where(head_of_lane == h, vb, zero)
                               for h in range(N_HEADS_A)], axis=0)
        sp = _dot(wcat_ref[...], rhs) + bsp_ref[...]
        ya = (u * sp * gate_a).astype(_BF16)

        (sp_, rp_), (sc_, rc_), (sn_, rn_) = band_at
        ywin = jnp.concatenate([y_r[sp_, rp_.stop - CONV_HALO:rp_.stop, :], y_r[sc_, rc_, :],
                                y_r[sn_, rn_.start:rn_.start + CONV_HALO, :]], axis=0)
        bg = bg_r[hs, here, :]
        span = CONV_ROWS + 2 * CONV_HALO
        yb_parts = []
        for ci in range(BLOCK // CONV_ROWS):
            issue()
            base = ci * CONV_ROWS
            ych = ywin[base:base + span, :]
            acc = jnp.zeros((CONV_ROWS, W_B), _F32)
            for r in range(SUBLANES):
                zr = ych if r == 0 else pltpu.roll(ych, span - r, axis=0)
                for m in range(2 * CONV_HALO // SUBLANES):
                    k = SUBLANES * m + r - 1
                    if 0 <= k < CONV_WIDTH:
                        acc = acc + zr[SUBLANES * m:SUBLANES * m + CONV_ROWS, :] * cw_ref[k:k + 1, :]
            z = _layer_norm(acc + vec(V_CONV_B, W_B), vec(V_CONV_LN_G, W_B), vec(V_CONV_LN_B, W_B))
            gate_b = jax.nn.silu(bg[base:base + CONV_ROWS, :])
            yb_parts.append((jax.nn.silu(z) * gate_b).astype(_BF16))
        yb = jnp.concatenate(yb_parts, axis=0)

        edge = jnp.where(g == 0, 1, jnp.where(g == nblk_seq - 1, 2, 0))
        q_blk = q_r[hs, here, :]
        cg = cg_r[hs, here, :]

        def band(var):
            return [kvar_r[s_, var, r_, :] for (s_, r_) in band_at]

        yc_parts = []
        for kvh in range(N_KV_HEADS):
            issue()
            qg = q_blk[:, kvh * 2 * LANES:(kvh + 1) * 2 * LANES]
            lhs = jnp.concatenate([qg[:, 0:LANES], qg[:, LANES:2 * LANES]], axis=0)
            kcat = jnp.concatenate(band(2 * kvh) + band(2 * kvh + 1), axis=0)
            s2 = lax.dot_general(lhs, kcat, (((1,), (1,)), ((), ())),
                                 preferred_element_type=_F32)
            p_rows = []
            sink_w = []
            for pp in range(2):
                p_pair = []
                for hh in range(2):
                    h = 4 * kvh + 2 * pp + hh
                    s = (s2[pp * BLOCK:(pp + 1) * BLOCK, hh * 3 * BLOCK:(hh + 1) * 3 * BLOCK]
                         + biasm_ref[edge, h])
                    sk = sink_ref[layer, h] * LOG2_E
                    m = jnp.maximum(jnp.max(s, axis=-1, keepdims=True), sk)
                    p_pair.append(jnp.exp2((s - m).astype(_BF16)))
                    sink_w.append(jnp.exp2(sk - m))
                p_rows.append(jnp.concatenate(p_pair, axis=1))
            pcat = jnp.concatenate(p_rows, axis=0)
            vcat = jnp.concatenate(
                [jnp.concatenate(band(4 + 2 * kvh) + band(5 + 2 * kvh), axis=0), ones_ref[...]], axis=1)
            o2 = _dot(pcat, vcat)
            for pp in range(2):
                rsl = slice(pp * BLOCK, (pp + 1) * BLOCK)
                cols = slice((2 * kvh + pp) * LANES, (2 * kvh + pp + 1) * LANES)
                den = o2[rsl, LANES:2 * LANES] + jnp.where(low, sink_w[2 * pp], sink_w[2 * pp + 1])
                o = o2[rsl, 0:LANES] / den
                yc_parts.append((o * jax.nn.silu(cg[:, cols])).astype(_BF16))
        return jnp.concatenate([ya, yb] + yc_parts, axis=1)

    def mix(jg, p, pieces):
        pieces = list(pieces)

        def issue(n=1):
            for _ in range(n):
                if pieces:
                    pieces.pop(0)()
        rows = pl.ds(pl.multiple_of(jg * GROUP_ROWS, GROUP_ROWS), GROUP_ROWS)
        hs = p & (H_SLOTS - 1)
        issue(2)
        ymix = jnp.concatenate([mix_block(b, p, issue) for b in range(GROUP_BLOCKS)], axis=0)

        issue(len(pieces))
        y_out = _dot(ymix, wout_ref[...])
        x_res = xres_r[hs] if first_layer else xc_ref[rows, :]
        out_ref[rows, :] = _layer_norm(DEEPNORM_ALPHA * x_res + y_out, vec(V_POST_LN_G), vec(V_POST_LN_B))

    p0 = i * n_groups

    @pl.when(i == 0)
    def _():
        kvar_r[BAND_SLOTS - 1] = jnp.zeros(kvar_r.shape[1:], _BF16)
        y_r[BAND_SLOTS - 1] = jnp.zeros(y_r.shape[1:], _F32)
        for piece in project_pieces(xc_ref[0:GROUP_ROWS, :], p0):
            piece()

    def body(jg, carry):
        p = p0 + jg
        if n_groups == 1:
            x_next = xn_ref[...]
        else:
            nxt = pl.multiple_of(jnp.minimum(jg + 1, n_groups - 1) * GROUP_ROWS, GROUP_ROWS)
            x_next = jnp.where(jg + 1 < n_groups, xc_ref[pl.ds(nxt, GROUP_ROWS), :], xn_ref[...])
        mix(jg, p, project_pieces(x_next, p + 1))
        return carry

    lax.fori_loop(0, n_groups, body, 0)


def _t5_bucket(rel):
    nb = N_BUCKETS // 2
    max_exact = nb // 2
    ret = jnp.where(rel > 0, nb, 0)
    n = jnp.abs(rel)
    nf = jnp.maximum(n, 1).astype(jnp.float32)
    large = max_exact + (jnp.log(nf / max_exact) / math.log(MAX_DISTANCE / max_exact)
                         * (nb - max_exact)).astype(jnp.int32)
    large = jnp.minimum(large, nb - 1)
    return ret + jnp.where(n < max_exact, n, large)


def _const_spec(shape):
    zeros = (0,) * len(shape)
    return pl.BlockSpec(shape, lambda b, i: zeros, pipeline_mode=pl.Buffered(1))


def _layer_slice_spec(stacked, layer):
    return pl.BlockSpec((None,) + stacked.shape[1:], lambda b, i: (layer, 0, 0),
                        pipeline_mode=pl.Buffered(1))


def _layer_call(layer, x, vecs, win, wcat, bsp, cw, sink, biasm, ones, wout):
    first_layer = layer == 0
    bsz, seq, d = x.shape
    tm = TILE_ROWS
    assert seq % tm == 0 and tm % GROUP_ROWS == 0 and seq // GROUP_ROWS >= 2
    groups_per_tile = tm // GROUP_ROWS
    nblk_seq = seq // BLOCK
    grid = (bsz, seq // tm)

    in_specs = [
        pl.BlockSpec((None, tm, d), lambda b, i: (b, i, 0)),
        pl.BlockSpec((None, GROUP_ROWS, d),
                     lambda b, i: (b, jnp.minimum((i + 1) * groups_per_tile,
                                                  seq // GROUP_ROWS - 1), 0)),
        _layer_slice_spec(vecs, layer), _layer_slice_spec(win, layer),
        _layer_slice_spec(wcat, layer), _layer_slice_spec(bsp, layer), _layer_slice_spec(cw, layer),
        pl.BlockSpec(memory_space=pltpu.SMEM),
        _const_spec(biasm.shape), _const_spec(ones.shape), _layer_slice_spec(wout, layer),
    ]
    scratch = [
        pltpu.VMEM((GROUP_ROWS, d), _BF16),
        pltpu.VMEM((H_SLOTS, GROUP_ROWS if first_layer else SUBLANES, d), _F32),
        pltpu.VMEM((H_SLOTS, GROUP_ROWS, 3 * W_A), _F32),
        pltpu.VMEM((H_SLOTS, GROUP_ROWS, W_B), _F32),
        pltpu.VMEM((H_SLOTS, GROUP_ROWS, W_C), _BF16),
        pltpu.VMEM((H_SLOTS, GROUP_ROWS, W_C), _F32),
        pltpu.VMEM((BAND_SLOTS, 8, GROUP_ROWS, LANES), _BF16),
        pltpu.VMEM((BAND_SLOTS, GROUP_ROWS, W_B), _F32),
    ]
    return pl.pallas_call(
        functools.partial(_layer_kernel, layer, tm, nblk_seq),
        grid=grid,
        in_specs=in_specs,
        out_specs=pl.BlockSpec((None, tm, d), lambda b, i: (b, i, 0)),
        out_shape=jax.ShapeDtypeStruct(x.shape, x.dtype),
        scratch_shapes=scratch,
        compiler_params=pltpu.CompilerParams(
            dimension_semantics=("arbitrary", "arbitrary"),
            vmem_limit_bytes=VMEM_LIMIT_BYTES),
        name="layer_first" if first_layer else "layer_next",
    )(x, x, vecs, win, wcat, bsp, cw, sink, biasm, ones, wout)


def kernel(x, ln_in_g, ln_in_b, w_in, gmlp_ln_g, gmlp_ln_b, w_spatial, b_spatial, conv_w, conv_b,
           conv_ln_g, conv_ln_b, attn_sink, rel_bias, w_out, post_ln_g, post_ln_b):
    depth = w_in.shape[0]
    assert depth == DEPTH and x.shape[2] == D_MODEL and w_in.shape[2] == D_IN

    qq = jnp.arange(BLOCK)[:, None]
    kk = jnp.arange(3 * BLOCK)[None, :]
    bucket = _t5_bucket(kk - BLOCK - qq).astype(jnp.int32)
    biasm = pl.pallas_call(
        _bias_kernel,
        in_specs=[pl.BlockSpec(memory_space=pltpu.VMEM), pl.BlockSpec(memory_space=pltpu.SMEM)],
        out_specs=pl.BlockSpec(memory_space=pltpu.VMEM),
        out_shape=jax.ShapeDtypeStruct((3, N_Q_HEADS, BLOCK, 3 * BLOCK), _F32),
        name="rel_bias_tables",
    )(bucket, rel_bias.astype(_F32))

    kidx = jnp.arange(6 * BLOCK)[:, None] // (3 * BLOCK)
    ones = (kidx == jnp.arange(LANES)[None, :] // HEAD_DIM).astype(_BF16)

    def vec_rows(*vectors):
        rows = [jnp.pad(v.astype(_F32), ((0, 0), (0, D_MODEL - v.shape[-1]))) for v in vectors]
        rows.append(jnp.zeros((depth, D_MODEL * (VEC_ROWS - len(rows))), _F32))
        return jnp.concatenate(rows, axis=1).reshape(depth, VEC_ROWS, D_MODEL)

    tile_l = lambda v: jnp.broadcast_to(v[None, :], (depth, v.shape[0]))
    vecs = vec_rows(tile_l(ln_in_g), tile_l(ln_in_b), post_ln_g, post_ln_b, gmlp_ln_g, gmlp_ln_b,
                    conv_b, conv_ln_g, conv_ln_b)
    win_all = w_in.astype(_BF16)
    wout_all = w_out.astype(_BF16)
    wcat = jnp.transpose(w_spatial, (0, 2, 1, 3)).reshape(depth, BLOCK, N_HEADS_A * BLOCK).astype(_BF16)
    bsp = jnp.repeat(jnp.transpose(b_spatial, (0, 2, 1)), HEAD_DIM, axis=2).astype(_F32)
    cw = conv_w.astype(_F32)
    sink = attn_sink.astype(_F32)
    for l in range(depth):
        x = _layer_call(l, x, vecs, win_all, wcat, bsp, cw, sink, biasm, ones, wout_all)
    return x
```

```python
import functools
import math

import jax
import jax.numpy as jnp
from jax import lax
from jax.experimental import pallas as pl
from jax.experimental.pallas import tpu as pltpu

D_MODEL = 1024
HEAD_DIM = 64
W_A = 256
W_B = 256
W_C = 512
N_HEADS_A = 4
N_Q_HEADS = 8
N_KV_HEADS = 2
KV_W = N_KV_HEADS * HEAD_DIM
CONV_WIDTH = 31
CONV_PAD = CONV_WIDTH // 2
BLOCK = 128
N_BUCKETS = 32
MAX_DISTANCE = 128
LN_EPS = 1e-5
NEG_INF = -1e30
DEPTH = 2
DEEPNORM_ALPHA = (2 * DEPTH) ** 0.25
LOG2_E = math.log2(math.e)
Q_SCALE = HEAD_DIM ** -0.5 * LOG2_E

COL_A = 0
COL_B = 3 * W_A
COL_Q = COL_B + 3 * W_B
COL_KV = COL_Q + W_C
COL_CG = COL_KV + 2 * KV_W
D_IN = COL_CG + W_C

LANES = 128
SUBLANES = 8
CONV_HALO = 16
CONV_ROWS = 64
GROUP_BLOCKS = 4
GROUP_ROWS = GROUP_BLOCKS * BLOCK
H_SLOTS = 2
BAND_SLOTS = 4
TILE_ROWS = 2 * GROUP_ROWS
VMEM_LIMIT_BYTES = 60 * 1024 * 1024

(V_LN_IN_G, V_LN_IN_B, V_POST_LN_G, V_POST_LN_B, V_GMLP_LN_G, V_GMLP_LN_B,
 V_CONV_B, V_CONV_LN_G, V_CONV_LN_B) = range(9)
VEC_ROWS = 16

_BF16 = jnp.bfloat16
_F32 = jnp.float32


def _layer_norm(x, g, b):
    mu = jnp.mean(x, axis=-1, keepdims=True)
    xc = x - mu
    var = jnp.mean(xc * xc, axis=-1, keepdims=True)
    return xc * lax.rsqrt(var + LN_EPS) * g + b


def _dot(a, b):
    return jnp.dot(a, b, preferred_element_type=_F32)


def _bias_kernel(bucket_ref, rb_ref, out_ref):
    bucket = bucket_ref[...]
    row = lax.broadcasted_iota(jnp.int32, (BLOCK, 3 * BLOCK), 0)
    col = lax.broadcasted_iota(jnp.int32, (BLOCK, 3 * BLOCK), 1)
    in_window = jnp.abs(col - BLOCK - row) <= BLOCK
    for h in range(N_Q_HEADS):
        acc = jnp.zeros((BLOCK, 3 * BLOCK), _F32)
        for b in range(N_BUCKETS):
            acc = jnp.where(bucket == b, rb_ref[b, h] * LOG2_E, acc)
        base = jnp.where(in_window, acc, NEG_INF)
        out_ref[0, h] = base
        out_ref[1, h] = jnp.where(col < BLOCK, NEG_INF, base)
        out_ref[2, h] = jnp.where(col >= 2 * BLOCK, NEG_INF, base)


def _layer_kernel(layer, tm, nblk_seq,
                  xc_ref, xn_ref, vec_ref, win_ref, wcat_ref, bsp_ref, cw_ref,
                  sink_ref, biasm_ref, ones_ref, wout_ref,
                  out_ref,
                  xb_s, xres_r, ha_r, bg_r, q_r, cg_r, kvar_r, y_r):
    first_layer = layer == 0
    i = pl.program_id(1)
    n_groups = tm // GROUP_ROWS
    n_groups_seq = nblk_seq // GROUP_BLOCKS
    low = lax.broadcasted_iota(jnp.int32, (1, LANES), 1) < HEAD_DIM
    head_of_lane = lax.broadcasted_iota(jnp.int32, (1, W_A), 1) // HEAD_DIM

    def vec(row, width=D_MODEL):
        return vec_ref[row:row + 1, 0:width]

    def project_pieces(x_grp, p_new):
        hs = p_new & (H_SLOTS - 1)
        bs = p_new & (BAND_SLOTS - 1)
        if first_layer:
            x_grp = _layer_norm(x_grp, vec(V_LN_IN_G), vec(V_LN_IN_B))
            xres_r[hs] = x_grp
        xb_s[...] = x_grp.astype(_BF16)

        def piece_a():
            ha_r[hs] = _dot(xb_s[...], win_ref[:, COL_A:COL_B])

        def piece_b():
            hb = _dot(xb_s[...], win_ref[:, COL_B:COL_Q])
            bg_r[hs] = hb[:, 2 * W_B:3 * W_B]
            y = hb[:, 0:W_B] * jax.nn.sigmoid(hb[:, W_B:2 * W_B])
            y_r[bs] = jnp.where(p_new < n_groups_seq, y, 0.0)

        def piece_q():
            q_r[hs] = (_dot(xb_s[...], win_ref[:, COL_Q:COL_KV]) * Q_SCALE).astype(_BF16)

        def piece_kv():
            kv = _dot(xb_s[...], win_ref[:, COL_KV:COL_CG])
            zero = jnp.zeros((GROUP_ROWS, LANES), _BF16)
            for t in range(2):
                nat = kv[:, t * KV_W:(t + 1) * KV_W]
                swp = pltpu.roll(nat, HEAD_DIM, axis=1).astype(_BF16)
                nat = nat.astype(_BF16)
                kvar_r[bs, 4 * t + 0] = jnp.where(low, nat, zero)
                kvar_r[bs, 4 * t + 1] = jnp.where(low, zero, swp)
                kvar_r[bs, 4 * t + 2] = jnp.where(low, swp, zero)
                kvar_r[bs, 4 * t + 3] = jnp.where(low, zero, nat)

        def piece_cg():
            cg_r[hs] = _dot(xb_s[...], win_ref[:, COL_CG:D_IN])

        return [piece_kv, piece_b, piece_q, piece_cg, piece_a]

    def mix_block(b, p, issue):
        hs = p & (H_SLOTS - 1)
        s_here = p & (BAND_SLOTS - 1)
        blk = [slice(k * BLOCK, (k + 1) * BLOCK) for k in range(GROUP_BLOCKS)]
        here = blk[b]
        before = ((p + BAND_SLOTS - 1) & (BAND_SLOTS - 1), blk[-1]) if b == 0 else (s_here, blk[b - 1])
        after = ((p + 1) & (BAND_SLOTS - 1), blk[0]) if b == GROUP_BLOCKS - 1 else (s_here, blk[b + 1])
        band_at = [before, (s_here, here), after]
        g = GROUP_BLOCKS * p + b

        ha = ha_r[hs, here, :]
        u = jax.nn.gelu(ha[:, 0:W_A])
        v = _layer_norm(jax.nn.gelu(ha[:, W_A:2 * W_A]), vec(V_GMLP_LN_G, W_A), vec(V_GMLP_LN_B, W_A))
        gate_a = jax.nn.silu(ha[:, 2 * W_A:3 * W_A])
        vb = v.astype(_BF16)
        zero = jnp.zeros_like(vb)
        rhs = jnp.concatenate([jnp.where(head_of_lane == h, vb, zero)
                               for h in range(N_HEADS_A)], axis=0)
        sp = _dot(wcat_ref[...], rhs) + bsp_ref[...]
        ya = (u * sp * gate_a).astype(_BF16)

        (sp_, rp_), (sc_, rc_), (sn_, rn_) = band_at
        ywin = jnp.concatenate([y_r[sp_, rp_.stop - CONV_HALO:rp_.stop, :], y_r[sc_, rc_, :],
                                y_r[sn_, rn_.start:rn_.start + CONV_HALO, :]], axis=0)
        bg = bg_r[hs, here, :]
        span = CONV_ROWS + 2 * CONV_HALO
        yb_parts = []
        for ci in range(BLOCK // CONV_ROWS):
            issue()
            base = ci * CONV_ROWS
            ych = ywin[base:base + span, :]
            acc = jnp.zeros((CONV_ROWS, W_B), _F32)
            for r in range(SUBLANES):
                zr = ych if r == 0 else pltpu.roll(ych, span - r, axis=0)
                for m in range(2 * CONV_HALO // SUBLANES):
                    k = SUBLANES * m + r - 1
                    if 0 <= k < CONV_WIDTH:
                        acc = acc + zr[SUBLANES * m:SUBLANES * m + CONV_ROWS, :] * cw_ref[k:k + 1, :]
            z = _layer_norm(acc + vec(V_CONV_B, W_B), vec(V_CONV_LN_G, W_B), vec(V_CONV_LN_B, W_B))
            gate_b = jax.nn.silu(bg[base:base + CONV_ROWS, :])
            yb_parts.append((jax.nn.silu(z) * gate_b).astype(_BF16))
        yb = jnp.concatenate(yb_parts, axis=0)

        edge = jnp.where(g == 0, 1, jnp.where(g == nblk_seq - 1, 2, 0))
        q_blk = q_r[hs, here, :]
        cg = cg_r[hs, here, :]

        def band(var):
            return [kvar_r[s_, var, r_, :] for (s_, r_) in band_at]

        yc_parts = []
        for kvh in range(N_KV_HEADS):
            issue()
            qg = q_blk[:, kvh * 2 * LANES:(kvh + 1) * 2 * LANES]
            lhs = jnp.concatenate([qg[:, 0:LANES], qg[:, LANES:2 * LANES]], axis=0)
            kcat = jnp.concatenate(band(2 * kvh) + band(2 * kvh + 1), axis=0)
            s2 = lax.dot_general(lhs, kcat, (((1,), (1,)), ((), ())),
                                 preferred_element_type=_F32)
            p_rows = []
            sink_w = []
            for pp in range(2):
                p_pair = []
                for hh in range(2):
                    h = 4 * kvh + 2 * pp + hh
                    s = (s2[pp * BLOCK:(pp + 1) * BLOCK, hh * 3 * BLOCK:(hh + 1) * 3 * BLOCK]
                         + biasm_ref[edge, h])
                    sk = sink_ref[layer, h] * LOG2_E
                    m = jnp.maximum(jnp.max(s, axis=-1, keepdims=True), sk)
                    p_pair.append(jnp.exp2((s - m).astype(_BF16)))
                    sink_w.append(jnp.exp2(sk - m))
                p_rows.append(jnp.concatenate(p_pair, axis=1))
            pcat = jnp.concatenate(p_rows, axis=0)
            vcat = jnp.concatenate(
                [jnp.concatenate(band(4 + 2 * kvh) + band(5 + 2 * kvh), axis=0), ones_ref[...]], axis=1)
            o2 = _dot(pcat, vcat)
            for pp in range(2):
                rsl = slice(pp * BLOCK, (pp + 1) * BLOCK)
                cols = slice((2 * kvh + pp) * LANES, (2 * kvh + pp + 1) * LANES)
                den = o2[rsl, LANES:2 * LANES] + jnp.where(low, sink_w[2 * pp], sink_w[2 * pp + 1])
                o = o2[rsl, 0:LANES] / den
                yc_parts.append((o * jax.nn.silu(cg[:, cols])).astype(_BF16))
        return jnp.concatenate([ya, yb] + yc_parts, axis=1)

    def mix(jg, p, pieces):
        pieces = list(pieces)

        def issue(n=1):
            for _ in range(n):
                if pieces:
                    pieces.pop(0)()
        rows = pl.ds(pl.multiple_of(jg * GROUP_ROWS, GROUP_ROWS), GROUP_ROWS)
        hs = p & (H_SLOTS - 1)
        issue(2)
        ymix = jnp.concatenate([mix_block(b, p, issue) for b in range(GROUP_BLOCKS)], axis=0)

        issue(len(pieces))
        y_out = _dot(ymix, wout_ref[...])
        x_res = xres_r[hs] if first_layer else xc_ref[rows, :]
        out_ref[rows, :] = _layer_norm(DEEPNORM_ALPHA * x_res + y_out, vec(V_POST_LN_G), vec(V_POST_LN_B))

    p0 = i * n_groups

    @pl.when(i == 0)
    def _():
        kvar_r[BAND_SLOTS - 1] = jnp.zeros(kvar_r.shape[1:], _BF16)
        y_r[BAND_SLOTS - 1] = jnp.zeros(y_r.shape[1:], _F32)
        for piece in project_pieces(xc_ref[0:GROUP_ROWS, :], p0):
            piece()

    def body(jg, carry):
        p = p0 + jg
        if n_groups == 1:
            x_next = xn_ref[...]
        else:
            nxt = pl.multiple_of(jnp.minimum(jg + 1, n_groups - 1) * GROUP_ROWS, GROUP_ROWS)
            x_next = jnp.where(jg + 1 < n_groups, xc_ref[pl.ds(nxt, GROUP_ROWS), :], xn_ref[...])
        mix(jg, p, project_pieces(x_next, p + 1))
        return carry

    lax.fori_loop(0, n_groups, body, 0)


def _t5_bucket(rel):
    nb = N_BUCKETS // 2
    max_exact = nb // 2
    ret = jnp.where(rel > 0, nb, 0)
    n = jnp.abs(rel)
    nf = jnp.maximum(n, 1).astype(jnp.float32)
    large = max_exact + (jnp.log(nf / max_exact) / math.log(MAX_DISTANCE / max_exact)
                         * (nb - max_exact)).astype(jnp.int32)
    large = jnp.minimum(large, nb - 1)
    return ret + jnp.where(n < max_exact, n, large)


def _const_spec(shape):
    zeros = (0,) * len(shape)
    return pl.BlockSpec(shape, lambda b, i: zeros, pipeline_mode=pl.Buffered(1))


def _layer_slice_spec(stacked, layer):
    return pl.BlockSpec((None,) + stacked.shape[1:], lambda b, i: (layer, 0, 0),
                        pipeline_mode=pl.Buffered(1))


def _layer_call(layer, x, vecs, win, wcat, bsp, cw, sink, biasm, ones, wout):
    first_layer = layer == 0
    bsz, seq, d = x.shape
    tm = TILE_ROWS
    assert seq % tm == 0 and tm % GROUP_ROWS == 0 and seq // GROUP_ROWS >= 2
    groups_per_tile = tm // GROUP_ROWS
    nblk_seq = seq // BLOCK
    grid = (bsz, seq // tm)

    in_specs = [
        pl.BlockSpec((None, tm, d), lambda b, i: (b, i, 0)),
        pl.BlockSpec((None, GROUP_ROWS, d),
                     lambda b, i: (b, jnp.minimum((i + 1) * groups_per_tile,
                                                  seq // GROUP_ROWS - 1), 0)),
        _layer_slice_spec(vecs, layer), _layer_slice_spec(win, layer),
        _layer_slice_spec(wcat, layer), _layer_slice_spec(bsp, layer), _layer_slice_spec(cw, layer),
        pl.BlockSpec(memory_space=pltpu.SMEM),
        _const_spec(biasm.shape), _const_spec(ones.shape), _layer_slice_spec(wout, layer),
    ]
    scratch = [
        pltpu.VMEM((GROUP_ROWS, d), _BF16),
        pltpu.VMEM((H_SLOTS, GROUP_ROWS if first_layer else SUBLANES, d), _F32),
        pltpu.VMEM((H_SLOTS, GROUP_ROWS, 3 * W_A), _F32),
        pltpu.VMEM((H_SLOTS, GROUP_ROWS, W_B), _F32),
        pltpu.VMEM((H_SLOTS, GROUP_ROWS, W_C), _BF16),
        pltpu.VMEM((H_SLOTS, GROUP_ROWS, W_C), _F32),
        pltpu.VMEM((BAND_SLOTS, 8, GROUP_ROWS, LANES), _BF16),
        pltpu.VMEM((BAND_SLOTS, GROUP_ROWS, W_B), _F32),
    ]
    return pl.pallas_call(
        functools.partial(_layer_kernel, layer, tm, nblk_seq),
        grid=grid,
        in_specs=in_specs,
        out_specs=pl.BlockSpec((None, tm, d), lambda b, i: (b, i, 0)),
        out_shape=jax.ShapeDtypeStruct(x.shape, x.dtype),
        scratch_shapes=scratch,
        compiler_params=pltpu.CompilerParams(
            dimension_semantics=("arbitrary", "arbitrary"),
            vmem_limit_bytes=VMEM_LIMIT_BYTES),
        name="layer_first" if first_layer else "layer_next",
    )(x, x, vecs, win, wcat, bsp, cw, sink, biasm, ones, wout)


def kernel(x, ln_in_g, ln_in_b, w_in, gmlp_ln_g, gmlp_ln_b, w_spatial, b_spatial, conv_w, conv_b,
           conv_ln_g, conv_ln_b, attn_sink, rel_bias, w_out, post_ln_g, post_ln_b):
    depth = w_in.shape[0]
    assert depth == DEPTH and x.shape[2] == D_MODEL and w_in.shape[2] == D_IN

    qq = jnp.arange(BLOCK)[:, None]
    kk = jnp.arange(3 * BLOCK)[None, :]
    bucket = _t5_bucket(kk - BLOCK - qq).astype(jnp.int32)
    biasm = pl.pallas_call(
        _bias_kernel,
        in_specs=[pl.BlockSpec(memory_space=pltpu.VMEM), pl.BlockSpec(memory_space=pltpu.SMEM)],
        out_specs=pl.BlockSpec(memory_space=pltpu.VMEM),
        out_shape=jax.ShapeDtypeStruct((3, N_Q_HEADS, BLOCK, 3 * BLOCK), _F32),
        name="rel_bias_tables",
    )(bucket, rel_bias.astype(_F32))

    kidx = jnp.arange(6 * BLOCK)[:, None] // (3 * BLOCK)
    ones = (kidx == jnp.arange(LANES)[None, :] // HEAD_DIM).astype(_BF16)

    def vec_rows(*vectors):
        rows = [jnp.pad(v.astype(_F32), ((0, 0), (0, D_MODEL - v.shape[-1]))) for v in vectors]
        rows.append(jnp.zeros((depth, D_MODEL * (VEC_ROWS - len(rows))), _F32))
        return jnp.concatenate(rows, axis=1).reshape(depth, VEC_ROWS, D_MODEL)

    tile_l = lambda v: jnp.broadcast_to(v[None, :], (depth, v.shape[0]))
    vecs = vec_rows(tile_l(ln_in_g), tile_l(ln_in_b), post_ln_g, post_ln_b, gmlp_ln_g, gmlp_ln_b,
                    conv_b, conv_ln_g, conv_ln_b)
    win_all = w_in.astype(_BF16)
    wout_all = w_out.astype(_BF16)
    wcat = jnp.transpose(w_spatial, (0, 2, 1, 3)).reshape(depth, BLOCK, N_HEADS_A * BLOCK).astype(_BF16)
    bsp = jnp.repeat(jnp.transpose(b_spatial, (0, 2, 1)), HEAD_DIM, axis=2).astype(_F32)
    cw = conv_w.astype(_F32)
    sink = attn_sink.astype(_F32)
    for l in range(depth):
        x = _layer_call(l, x, vecs, win_all, wcat, bsp, cw, sink, biasm, ones, wout_all)
    return x
```

```python
import functools
import math

import jax
import jax.numpy as jnp
from jax import lax
from jax.experimental import pallas as pl
from jax.experimental.pallas import tpu as pltpu

D_MODEL = 1024
HEAD_DIM = 64
W_A = 256
W_B = 256
W_C = 512
N_HEADS_A = 4
N_Q_HEADS = 8
N_KV_HEADS = 2
KV_W = N_KV_HEADS * HEAD_DIM
CONV_WIDTH = 31
CONV_PAD = CONV_WIDTH // 2
BLOCK = 128
N_BUCKETS = 32
MAX_DISTANCE = 128
LN_EPS = 1e-5
NEG_INF = -1e30
DEPTH = 2
DEEPNORM_ALPHA = (2 * DEPTH) ** 0.25
LOG2_E = math.log2(math.e)
Q_SCALE = HEAD_DIM ** -0.5 * LOG2_E

COL_A = 0
COL_B = 3 * W_A
COL_Q = COL_B + 3 * W_B
COL_KV = COL_Q + W_C
COL_CG = COL_KV + 2 * KV_W
D_IN = COL_CG + W_C

LANES = 128
SUBLANES = 8
CONV_HALO = 16
CONV_ROWS = 64
GROUP_BLOCKS = 4
GROUP_ROWS = GROUP_BLOCKS * BLOCK
H_SLOTS = 2
BAND_SLOTS = 4
TILE_ROWS = GROUP_ROWS
VMEM_LIMIT_BYTES = 60 * 1024 * 1024

(V_LN_IN_G, V_LN_IN_B, V_POST_LN_G, V_POST_LN_B, V_GMLP_LN_G, V_GMLP_LN_B,
 V_CONV_B, V_CONV_LN_G, V_CONV_LN_B) = range(9)
VEC_ROWS = 16

_BF16 = jnp.bfloat16
_F32 = jnp.float32


def _layer_norm(x, g, b):
    mu = jnp.mean(x, axis=-1, keepdims=True)
    xc = x - mu
    var = jnp.mean(xc * xc, axis=-1, keepdims=True)
    return xc * lax.rsqrt(var + LN_EPS) * g + b


def _dot(a, b):
    return jnp.dot(a, b, preferred_element_type=_F32)


def _bias_kernel(bucket_ref, rb_ref, out_ref):
    bucket = bucket_ref[...]
    row = lax.broadcasted_iota(jnp.int32, (BLOCK, 3 * BLOCK), 0)
    col = lax.broadcasted_iota(jnp.int32, (BLOCK, 3 * BLOCK), 1)
    in_window = jnp.abs(col - BLOCK - row) <= BLOCK
    for h in range(N_Q_HEADS):
        acc = jnp.zeros((BLOCK, 3 * BLOCK), _F32)
        for b in range(N_BUCKETS):
            acc = jnp.where(bucket == b, rb_ref[b, h] * LOG2_E, acc)
        base = jnp.where(in_window, acc, NEG_INF)
        out_ref[0, h] = base
        out_ref[1, h] = jnp.where(col < BLOCK, NEG_INF, base)
        out_ref[2, h] = jnp.where(col >= 2 * BLOCK, NEG_INF, base)


def _layer_kernel(layer, tm, nblk_seq,
                  xc_ref, xn_ref, vec_ref, win_ref, wcat_ref, bsp_ref, cw_ref,
                  sink_ref, biasm_ref, ones_ref, wout_ref,
                  out_ref,
                  xb_s, xres_r, ha_r, bg_r, q_r, cg_r, kvar_r, y_r):
    first_layer = layer == 0
    i = pl.program_id(1)
    n_groups = tm // GROUP_ROWS
    n_groups_seq = nblk_seq // GROUP_BLOCKS
    low = lax.broadcasted_iota(jnp.int32, (1, LANES), 1) < HEAD_DIM
    head_of_lane = lax.broadcasted_iota(jnp.int32, (1, W_A), 1) // HEAD_DIM

    def vec(row, width=D_MODEL):
        return vec_ref[row:row + 1, 0:width]

    def project_pieces(x_grp, p_new):
        hs = p_new & (H_SLOTS - 1)
        bs = p_new & (BAND_SLOTS - 1)
        if first_layer:
            x_grp = _layer_norm(x_grp, vec(V_LN_IN_G), vec(V_LN_IN_B))
            xres_r[hs] = x_grp
        xb_s[...] = x_grp.astype(_BF16)

        def piece_a():
            ha_r[hs] = _dot(xb_s[...], win_ref[:, COL_A:COL_B])

        def piece_b():
            hb = _dot(xb_s[...], win_ref[:, COL_B:COL_Q])
            bg_r[hs] = hb[:, 2 * W_B:3 * W_B]
            y = hb[:, 0:W_B] * jax.nn.sigmoid(hb[:, W_B:2 * W_B])
            y_r[bs] = jnp.where(p_new < n_groups_seq, y, 0.0)

        def piece_q():
            q_r[hs] = (_dot(xb_s[...], win_ref[:, COL_Q:COL_KV]) * Q_SCALE).astype(_BF16)

        def piece_kv():
            kv = _dot(xb_s[...], win_ref[:, COL_KV:COL_CG])
            zero = jnp.zeros((GROUP_ROWS, LANES), _BF16)
            for t in range(2):
                nat = kv[:, t * KV_W:(t + 1) * KV_W]
                swp = pltpu.roll(nat, HEAD_DIM, axis=1).astype(_BF16)
                nat = nat.astype(_BF16)
                kvar_r[bs, 4 * t + 0] = jnp.where(low, nat, zero)
                kvar_r[bs, 4 * t + 1] = jnp.where(low, zero, swp)
                kvar_r[bs, 4 * t + 2] = jnp.where(low, swp, zero)
                kvar_r[bs, 4 * t + 3] = jnp.where(low, zero, nat)

        def piece_cg():
            cg_r[hs] = _dot(xb_s[...], win_ref[:, COL_CG:D_IN])

        return [piece_kv, piece_b, piece_q, piece_cg, piece_a]

    def mix_block(b, p, issue):
        hs = p & (H_SLOTS - 1)
        s_here = p & (BAND_SLOTS - 1)
        blk = [slice(k * BLOCK, (k + 1) * BLOCK) for k in range(GROUP_BLOCKS)]
        here = blk[b]
        before = ((p + BAND_SLOTS - 1) & (BAND_SLOTS - 1), blk[-1]) if b == 0 else (s_here, blk[b - 1])
        after = ((p + 1) & (BAND_SLOTS - 1), blk[0]) if b == GROUP_BLOCKS - 1 else (s_here, blk[b + 1])
        band_at = [before, (s_here, here), after]
        g = GROUP_BLOCKS * p + b

        ha = ha_r[hs, here, :]
        u = jax.nn.gelu(ha[:, 0:W_A])
        v = _layer_norm(jax.nn.gelu(ha[:, W_A:2 * W_A]), vec(V_GMLP_LN_G, W_A), vec(V_GMLP_LN_B, W_A))
        gate_a = jax.nn.silu(ha[:, 2 * W_A:3 * W_A])
        vb = v.astype(_BF16)
        zero = jnp.zeros_like(vb)
        rhs = jnp.concatenate([jnp.where(head_of_lane == h, vb, zero)
                               for h in range(N_HEADS_A)], axis=0)
        sp = _dot(wcat_ref[...], rhs) + bsp_ref[...]
        ya = (u * sp * gate_a).astype(_BF16)

        (sp_, rp_), (sc_, rc_), (sn_, rn_) = band_at
        ywin = jnp.concatenate([y_r[sp_, rp_.stop - CONV_HALO:rp_.stop, :], y_r[sc_, rc_, :],
                                y_r[sn_, rn_.start:rn_.start + CONV_HALO, :]], axis=0)
        bg = bg_r[hs, here, :]
        span = CONV_ROWS + 2 * CONV_HALO
        yb_parts = []
        for ci in range(BLOCK // CONV_ROWS):
            issue()
            base = ci * CONV_ROWS
            acc_halves = []
            for lh in range(W_B // LANES):
                ych = ywin[base:base + span, lh * LANES:(lh + 1) * LANES]
                acc_h = jnp.zeros((CONV_ROWS, LANES), _F32)
                for r in range(SUBLANES):
                    zr = ych if r == 0 else pltpu.roll(ych, span - r, axis=0)
                    for m in range(2 * CONV_HALO // SUBLANES):
                        k = SUBLANES * m + r - 1
                        if 0 <= k < CONV_WIDTH:
                            acc_h = acc_h + (zr[SUBLANES * m:SUBLANES * m + CONV_ROWS, :]
                                             * cw_ref[k:k + 1, lh * LANES:(lh + 1) * LANES])
                acc_halves.append(acc_h)
            acc = jnp.concatenate(acc_halves, axis=1)
            z = _layer_norm(acc + vec(V_CONV_B, W_B), vec(V_CONV_LN_G, W_B), vec(V_CONV_LN_B, W_B))
            gate_b = jax.nn.silu(bg[base:base + CONV_ROWS, :])
            yb_parts.append((jax.nn.silu(z) * gate_b).astype(_BF16))
        yb = jnp.concatenate(yb_parts, axis=0)

        edge = jnp.where(g == 0, 1, jnp.where(g == nblk_seq - 1, 2, 0))
        q_blk = q_r[hs, here, :]
        cg = cg_r[hs, here, :]

        def band(var):
            return [kvar_r[s_, var, r_, :] for (s_, r_) in band_at]

        yc_parts = []
        for kvh in range(N_KV_HEADS):
            issue()
            qg = q_blk[:, kvh * 2 * LANES:(kvh + 1) * 2 * LANES]
            lhs = jnp.concatenate([qg[:, 0:LANES], qg[:, LANES:2 * LANES]], axis=0)
            kcat = jnp.concatenate(band(2 * kvh) + band(2 * kvh + 1), axis=0)
            s2 = lax.dot_general(lhs, kcat, (((1,), (1,)), ((), ())),
                                 preferred_element_type=_F32)
            p_rows = []
            sink_w = []
            for pp in range(2):
                p_pair = []
                for hh in range(2):
                    h = 4 * kvh + 2 * pp + hh
                    s = (s2[pp * BLOCK:(pp + 1) * BLOCK, hh * 3 * BLOCK:(hh + 1) * 3 * BLOCK]
                         + biasm_ref[edge, h])
                    sk = sink_ref[layer, h] * LOG2_E
                    m = jnp.maximum(jnp.max(s, axis=-1, keepdims=True), sk)
                    p_pair.append(jnp.exp2((s - m).astype(_BF16)))
                    sink_w.append(jnp.exp2(sk - m))
                p_rows.append(jnp.concatenate(p_pair, axis=1))
            pcat = jnp.concatenate(p_rows, axis=0)
            vcat = jnp.concatenate(
                [jnp.concatenate(band(4 + 2 * kvh) + band(5 + 2 * kvh), axis=0), ones_ref[...]], axis=1)
            o2 = _dot(pcat, vcat)
            for pp in range(2):
                rsl = slice(pp * BLOCK, (pp + 1) * BLOCK)
                cols = slice((2 * kvh + pp) * LANES, (2 * kvh + pp + 1) * LANES)
                den = o2[rsl, LANES:2 * LANES] + jnp.where(low, sink_w[2 * pp], sink_w[2 * pp + 1])
                o = o2[rsl, 0:LANES] / den
                yc_parts.append((o * jax.nn.silu(cg[:, cols])).astype(_BF16))
        return jnp.concatenate([ya, yb] + yc_parts, axis=1)

    def mix(jg, p, pieces):
        pieces = list(pieces)

        def issue(n=1):
            for _ in range(n):
                if pieces:
                    pieces.pop(0)()
        rows = pl.ds(pl.multiple_of(jg * GROUP_ROWS, GROUP_ROWS), GROUP_ROWS)
        hs = p & (H_SLOTS - 1)
        issue(2)
        ymix = jnp.concatenate([mix_block(b, p, issue) for b in range(GROUP_BLOCKS)], axis=0)

        issue(len(pieces))
        y_out = _dot(ymix, wout_ref[...])
        x_res = xres_r[hs] if first_layer else xc_ref[rows, :]
        out_ref[rows, :] = _layer_norm(DEEPNORM_ALPHA * x_res + y_out, vec(V_POST_LN_G), vec(V_POST_LN_B))

    p0 = i * n_groups

    @pl.when(i == 0)
    def _():
        kvar_r[BAND_SLOTS - 1] = jnp.zeros(kvar_r.shape[1:], _BF16)
        y_r[BAND_SLOTS - 1] = jnp.zeros(y_r.shape[1:], _F32)
        for piece in project_pieces(xc_ref[0:GROUP_ROWS, :], p0):
            piece()

    def body(jg, carry):
        p = p0 + jg
        if n_groups == 1:
            x_next = xn_ref[...]
        else:
            nxt = pl.multiple_of(jnp.minimum(jg + 1, n_groups - 1) * GROUP_ROWS, GROUP_ROWS)
            x_next = jnp.where(jg + 1 < n_groups, xc_ref[pl.ds(nxt, GROUP_ROWS), :], xn_ref[...])
        mix(jg, p, project_pieces(x_next, p + 1))
        return carry

    lax.fori_loop(0, n_groups, body, 0)


def _t5_bucket(rel):
    nb = N_BUCKETS // 2
    max_exact = nb // 2
    ret = jnp.where(rel > 0, nb, 0)
    n = jnp.abs(rel)
    nf = jnp.maximum(n, 1).astype(jnp.float32)
    large = max_exact + (jnp.log(nf / max_exact) / math.log(MAX_DISTANCE / max_exact)
                         * (nb - max_exact)).astype(jnp.int32)
    large = jnp.minimum(large, nb - 1)
    return ret + jnp.where(n < max_exact, n, large)


def _const_spec(shape):
    zeros = (0,) * len(shape)
    return pl.BlockSpec(shape, lambda b, i: zeros, pipeline_mode=pl.Buffered(1))


def _layer_slice_spec(stacked, layer):
    return pl.BlockSpec((None,) + stacked.shape[1:], lambda b, i: (layer, 0, 0),
                        pipeline_mode=pl.Buffered(1))


def _layer_call(layer, x, vecs, win, wcat, bsp, cw, sink, biasm, ones, wout):
    first_layer = layer == 0
    bsz, seq, d = x.shape
    tm = TILE_ROWS
    assert seq % tm == 0 and tm % GROUP_ROWS == 0 and seq // GROUP_ROWS >= 2
    groups_per_tile = tm // GROUP_ROWS
    nblk_seq = seq // BLOCK
    grid = (bsz, seq // tm)

    in_specs = [
        pl.BlockSpec((None, tm, d), lambda b, i: (b, i, 0)),
        pl.BlockSpec((None, GROUP_ROWS, d),
                     lambda b, i: (b, jnp.minimum((i + 1) * groups_per_tile,
                                                  seq // GROUP_ROWS - 1), 0)),
        _layer_slice_spec(vecs, layer), _layer_slice_spec(win, layer),
        _layer_slice_spec(wcat, layer), _layer_slice_spec(bsp, layer), _layer_slice_spec(cw, layer),
        pl.BlockSpec(memory_space=pltpu.SMEM),
        _const_spec(biasm.shape), _const_spec(ones.shape), _layer_slice_spec(wout, layer),
    ]
    scratch = [
        pltpu.VMEM((GROUP_ROWS, d), _BF16),
        pltpu.VMEM((H_SLOTS, GROUP_ROWS if first_layer else SUBLANES, d), _F32),
        pltpu.VMEM((H_SLOTS, GROUP_ROWS, 3 * W_A), _F32),
        pltpu.VMEM((H_SLOTS, GROUP_ROWS, W_B), _F32),
        pltpu.VMEM((H_SLOTS, GROUP_ROWS, W_C), _BF16),
        pltpu.VMEM((H_SLOTS, GROUP_ROWS, W_C), _F32),
        pltpu.VMEM((BAND_SLOTS, 8, GROUP_ROWS, LANES), _BF16),
        pltpu.VMEM((BAND_SLOTS, GROUP_ROWS, W_B), _F32),
    ]
    return pl.pallas_call(
        functools.partial(_layer_kernel, layer, tm, nblk_seq),
        grid=grid,
        in_specs=in_specs,
        out_specs=pl.BlockSpec((None, tm, d), lambda b, i: (b, i, 0)),
        out_shape=jax.ShapeDtypeStruct(x.shape, x.dtype),
        scratch_shapes=scratch,
        compiler_params=pltpu.CompilerParams(
            dimension_semantics=("arbitrary", "arbitrary"),
            vmem_limit_bytes=VMEM_LIMIT_BYTES),
        name="layer_first" if first_layer else "layer_next",
    )(x, x, vecs, win, wcat, bsp, cw, sink, biasm, ones, wout)


def kernel(x, ln_in_g, ln_in_b, w_in, gmlp_ln_g, gmlp_ln_b, w_spatial, b_spatial, conv_w, conv_b,
           conv_ln_g, conv_ln_b, attn_sink, rel_bias, w_out, post_ln_g, post_ln_b):
    depth = w_in.shape[0]
    assert depth == DEPTH and x.shape[2] == D_MODEL and w_in.shape[2] == D_IN

    qq = jnp.arange(BLOCK)[:, None]
    kk = jnp.arange(3 * BLOCK)[None, :]
    bucket = _t5_bucket(kk - BLOCK - qq).astype(jnp.int32)
    biasm = pl.pallas_call(
        _bias_kernel,
        in_specs=[pl.BlockSpec(memory_space=pltpu.VMEM), pl.BlockSpec(memory_space=pltpu.SMEM)],
        out_specs=pl.BlockSpec(memory_space=pltpu.VMEM),
        out_shape=jax.ShapeDtypeStruct((3, N_Q_HEADS, BLOCK, 3 * BLOCK), _F32),
        name="rel_bias_tables",
    )(bucket, rel_bias.astype(_F32))

    kidx = jnp.arange(6 * BLOCK)[:, None] // (3 * BLOCK)
    ones = (kidx == jnp.arange(LANES)[None, :] // HEAD_DIM).astype(_BF16)

    def vec_rows(*vectors):
        rows = [jnp.pad(v.astype(_F32), ((0, 0), (0, D_MODEL - v.shape[-1]))) for v in vectors]
        rows.append(jnp.zeros((depth, D_MODEL * (VEC_ROWS - len(rows))), _F32))
        return jnp.concatenate(rows, axis=1).reshape(depth, VEC_ROWS, D_MODEL)

    tile_l = lambda v: jnp.broadcast_to(v[None, :], (depth, v.shape[0]))
    vecs = vec_rows(tile_l(ln_in_g), tile_l(ln_in_b), post_ln_g, post_ln_b, gmlp_ln_g, gmlp_ln_b,
                    conv_b, conv_ln_g, conv_ln_b)
    win_all = w_in.astype(_BF16)
    wout_all = w_out.astype(_BF16)
    wcat = jnp.transpose(w_spatial, (0, 2, 1, 3)).reshape(depth, BLOCK, N_HEADS_A * BLOCK).astype(_BF16)
    bsp = jnp.repeat(jnp.transpose(b_spatial, (0, 2, 1)), HEAD_DIM, axis=2).astype(_F32)
    cw = conv_w.astype(_F32)
    sink = attn_sink.astype(_F32)
    for l in range(depth):
        x = _layer_call(l, x, vecs, win_all, wcat, bsp, cw, sink, biasm, ones, wout_all)
    return x
```

```python
import functools
import math

import jax
import jax.numpy as jnp
from jax import lax
from jax.experimental import pallas as pl
from jax.experimental.pallas import tpu as pltpu

D_MODEL = 1024
HEAD_DIM = 64
W_A = 256
W_B = 256
W_C = 512
N_HEADS_A = 4
N_Q_HEADS = 8
N_KV_HEADS = 2
KV_W = N_KV_HEADS * HEAD_DIM
CONV_WIDTH = 31
CONV_PAD = CONV_WIDTH // 2
BLOCK = 128
N_BUCKETS = 32
MAX_DISTANCE = 128
LN_EPS = 1e-5
NEG_INF = -1e30
DEPTH = 2
DEEPNORM_ALPHA = (2 * DEPTH) ** 0.25
LOG2_E = math.log2(math.e)
Q_SCALE = HEAD_DIM ** -0.5 * LOG2_E

COL_A = 0
COL_B = 3 * W_A
COL_Q = COL_B + 3 * W_B
COL_KV = COL_Q + W_C
COL_CG = COL_KV + 2 * KV_W
D_IN = COL_CG + W_C

LANES = 128
SUBLANES = 8
CONV_HALO = 16
CONV_ROWS = 64
GROUP_BLOCKS = 4
GROUP_ROWS = GROUP_BLOCKS * BLOCK
H_SLOTS = 2
BAND_SLOTS = 4
TILE_ROWS = GROUP_ROWS
VMEM_LIMIT_BYTES = 60 * 1024 * 1024

(V_LN_IN_G, V_LN_IN_B, V_POST_LN_G, V_POST_LN_B, V_GMLP_LN_G, V_GMLP_LN_B,
 V_CONV_B, V_CONV_LN_G, V_CONV_LN_B) = range(9)
VEC_ROWS = 16

_BF16 = jnp.bfloat16
_F32 = jnp.float32


def _layer_norm(x, g, b):
    mu = jnp.mean(x, axis=-1, keepdims=True)
    xc = x - mu
    var = jnp.mean(xc * xc, axis=-1, keepdims=True)
    return xc * lax.rsqrt(var + LN_EPS) * g + b


def _dot(a, b):
    return jnp.dot(a, b, preferred_element_type=_F32)


def _bias_kernel(bucket_ref, rb_ref, out_ref):
    bucket = bucket_ref[...]
    row = lax.broadcasted_iota(jnp.int32, (BLOCK, 3 * BLOCK), 0)
    col = lax.broadcasted_iota(jnp.int32, (BLOCK, 3 * BLOCK), 1)
    in_window = jnp.abs(col - BLOCK - row) <= BLOCK
    for h in range(N_Q_HEADS):
        acc = jnp.zeros((BLOCK, 3 * BLOCK), _F32)
        for b in range(N_BUCKETS):
            acc = jnp.where(bucket == b, rb_ref[b, h] * LOG2_E, acc)
        base = jnp.where(in_window, acc, NEG_INF)
        out_ref[0, h] = base
        out_ref[1, h] = jnp.where(col < BLOCK, NEG_INF, base)
        out_ref[2, h] = jnp.where(col >= 2 * BLOCK, NEG_INF, base)


def _layer_kernel(layer, tm, nblk_seq,
                  xc_ref, xn_ref, vec_ref, win_ref, wcat_ref, bsp_ref, cw_ref,
                  sink_ref, biasm_ref, ones_ref, wout_ref,
                  out_ref,
                  xb_s, xres_r, ha_r, bg_r, q_r, cg_r, kvar_r, y_r):
    first_layer = layer == 0
    i = pl.program_id(1)
    n_groups = tm // GROUP_ROWS
    n_groups_seq = nblk_seq // GROUP_BLOCKS
    low = lax.broadcasted_iota(jnp.int32, (1, LANES), 1) < HEAD_DIM
    head_of_lane = lax.broadcasted_iota(jnp.int32, (1, W_A), 1) // HEAD_DIM

    def vec(row, width=D_MODEL):
        return vec_ref[row:row + 1, 0:width]

    def project_pieces(x_grp, p_new):
        hs = p_new & (H_SLOTS - 1)
        bs = p_new & (BAND_SLOTS - 1)
        if first_layer:
            x_grp = _layer_norm(x_grp, vec(V_LN_IN_G), vec(V_LN_IN_B))
            xres_r[hs] = x_grp
        xb_s[...] = x_grp.astype(_BF16)

        def piece_a():
            ha_r[hs] = _dot(xb_s[...], win_ref[:, COL_A:COL_B])

        def piece_b():
            hb = _dot(xb_s[...], win_ref[:, COL_B:COL_Q])
            bg_r[hs] = hb[:, 2 * W_B:3 * W_B]
            y = hb[:, 0:W_B] * jax.nn.sigmoid(hb[:, W_B:2 * W_B])
            y_r[bs] = jnp.where(p_new < n_groups_seq, y, 0.0)

        def piece_q():
            q_r[hs] = (_dot(xb_s[...], win_ref[:, COL_Q:COL_KV]) * Q_SCALE).astype(_BF16)

        def piece_kv():
            kv = _dot(xb_s[...], win_ref[:, COL_KV:COL_CG])
            zero = jnp.zeros((GROUP_ROWS, LANES), _BF16)
            for t in range(2):
                nat = kv[:, t * KV_W:(t + 1) * KV_W]
                swp = pltpu.roll(nat, HEAD_DIM, axis=1).astype(_BF16)
                nat = nat.astype(_BF16)
                kvar_r[bs, 4 * t + 0] = jnp.where(low, nat, zero)
                kvar_r[bs, 4 * t + 1] = jnp.where(low, zero, swp)
                kvar_r[bs, 4 * t + 2] = jnp.where(low, swp, zero)
                kvar_r[bs, 4 * t + 3] = jnp.where(low, zero, nat)

        def piece_cg():
            cg_r[hs] = _dot(xb_s[...], win_ref[:, COL_CG:D_IN])

        return [piece_kv, piece_b, piece_q, piece_cg, piece_a]

    def mix_block(b, p, issue):
        hs = p & (H_SLOTS - 1)
        s_here = p & (BAND_SLOTS - 1)
        blk = [slice(k * BLOCK, (k + 1) * BLOCK) for k in range(GROUP_BLOCKS)]
        here = blk[b]
        before = ((p + BAND_SLOTS - 1) & (BAND_SLOTS - 1), blk[-1]) if b == 0 else (s_here, blk[b - 1])
        after = ((p + 1) & (BAND_SLOTS - 1), blk[0]) if b == GROUP_BLOCKS - 1 else (s_here, blk[b + 1])
        band_at = [before, (s_here, here), after]
        g = GROUP_BLOCKS * p + b

        ha = ha_r[hs, here, :]
        u = jax.nn.gelu(ha[:, 0:W_A])
        v = _layer_norm(jax.nn.gelu(ha[:, W_A:2 * W_A]), vec(V_GMLP_LN_G, W_A), vec(V_GMLP_LN_B, W_A))
        gate_a = jax.nn.silu(ha[:, 2 * W_A:3 * W_A])
        vb = v.astype(_BF16)
        zero = jnp.zeros_like(vb)
        rhs = jnp.concatenate([jnp.where(head_of_lane == h, vb, zero)
                               for h in range(N_HEADS_A)], axis=0)
        sp = _dot(wcat_ref[...], rhs) + bsp_ref[...]
        ya = (u * sp * gate_a).astype(_BF16)

        (sp_, rp_), (sc_, rc_), (sn_, rn_) = band_at
        ywin = jnp.concatenate([y_r[sp_, rp_.stop - CONV_HALO:rp_.stop, :], y_r[sc_, rc_, :],
                                y_r[sn_, rn_.start:rn_.start + CONV_HALO, :]], axis=0)
        span = CONV_ROWS + 2 * CONV_HALO
        yb_parts = []
        for ci in range(BLOCK // CONV_ROWS):
            issue()
            base = ci * CONV_ROWS
            ych = ywin[base:base + span, :]
            acc = jnp.zeros((CONV_ROWS, W_B), _F32)
            for r in range(SUBLANES):
                zr = ych if r == 0 else pltpu.roll(ych, span - r, axis=0)
                for m in range(2 * CONV_HALO // SUBLANES):
                    k = SUBLANES * m + r - 1
                    if 0 <= k < CONV_WIDTH:
                        acc = acc + zr[SUBLANES * m:SUBLANES * m + CONV_ROWS, :] * cw_ref[k:k + 1, :]
            z = _layer_norm(acc + vec(V_CONV_B, W_B), vec(V_CONV_LN_G, W_B), vec(V_CONV_LN_B, W_B))
            gate_b = jax.nn.silu(bg_r[hs, here.start + base:here.start + base + CONV_ROWS, :])
            yb_parts.append((jax.nn.silu(z) * gate_b).astype(_BF16))
        yb = jnp.concatenate(yb_parts, axis=0)

        edge = jnp.where(g == 0, 1, jnp.where(g == nblk_seq - 1, 2, 0))

        def band(var):
            return [kvar_r[s_, var, r_, :] for (s_, r_) in band_at]

        yc_parts = []
        for kvh in range(N_KV_HEADS):
            issue()
            lhs = jnp.concatenate([q_r[hs, here, (2 * kvh + pp) * LANES:(2 * kvh + pp + 1) * LANES]
                                   for pp in range(2)], axis=0)
            kcat = jnp.concatenate(band(2 * kvh) + band(2 * kvh + 1), axis=0)
            s2 = lax.dot_general(lhs, kcat, (((1,), (1,)), ((), ())),
                                 preferred_element_type=_F32)
            p_rows = []
            sink_w = []
            for pp in range(2):
                p_pair = []
                for hh in range(2):
                    h = 4 * kvh + 2 * pp + hh
                    s = (s2[pp * BLOCK:(pp + 1) * BLOCK, hh * 3 * BLOCK:(hh + 1) * 3 * BLOCK]
                         + biasm_ref[edge, h])
                    sk = sink_ref[layer, h] * LOG2_E
                    m = jnp.maximum(jnp.max(s, axis=-1, keepdims=True), sk)
                    p_pair.append(jnp.exp2((s - m).astype(_BF16)))
                    sink_w.append(jnp.exp2(sk - m))
                p_rows.append(jnp.concatenate(p_pair, axis=1))
            pcat = jnp.concatenate(p_rows, axis=0)
            vcat = jnp.concatenate(
                [jnp.concatenate(band(4 + 2 * kvh) + band(5 + 2 * kvh), axis=0), ones_ref[...]], axis=1)
            o2 = _dot(pcat, vcat)
            for pp in range(2):
                rsl = slice(pp * BLOCK, (pp + 1) * BLOCK)
                cols = slice((2 * kvh + pp) * LANES, (2 * kvh + pp + 1) * LANES)
                den = o2[rsl, LANES:2 * LANES] + jnp.where(low, sink_w[2 * pp], sink_w[2 * pp + 1])
                o = o2[rsl, 0:LANES] / den
                yc_parts.append((o * jax.nn.silu(cg_r[hs, here, cols])).astype(_BF16))
        return jnp.concatenate([ya, yb] + yc_parts, axis=1)

    def mix(jg, p, pieces):
        pieces = list(pieces)

        def issue(n=1):
            for _ in range(n):
                if pieces:
                    pieces.pop(0)()
        rows = pl.ds(pl.multiple_of(jg * GROUP_ROWS, GROUP_ROWS), GROUP_ROWS)
        hs = p & (H_SLOTS - 1)
        issue(2)
        ymix = jnp.concatenate([mix_block(b, p, issue) for b in range(GROUP_BLOCKS)], axis=0)

        issue(len(pieces))
        y_out = _dot(ymix, wout_ref[...])
        x_res = xres_r[hs] if first_layer else xc_ref[rows, :]
        out_ref[rows, :] = _layer_norm(DEEPNORM_ALPHA * x_res + y_out, vec(V_POST_LN_G), vec(V_POST_LN_B))

    p0 = i * n_groups

    @pl.when(i == 0)
    def _():
        kvar_r[BAND_SLOTS - 1] = jnp.zeros(kvar_r.shape[1:], _BF16)
        y_r[BAND_SLOTS - 1] = jnp.zeros(y_r.shape[1:], _F32)
        for piece in project_pieces(xc_ref[0:GROUP_ROWS, :], p0):
            piece()

    def body(jg, carry):
        p = p0 + jg
        if n_groups == 1:
            x_next = xn_ref[...]
        else:
            nxt = pl.multiple_of(jnp.minimum(jg + 1, n_groups - 1) * GROUP_ROWS, GROUP_ROWS)
            x_next = jnp.where(jg + 1 < n_groups, xc_ref[pl.ds(nxt, GROUP_ROWS), :], xn_ref[...])
        mix(jg, p, project_pieces(x_next, p + 1))
        return carry

    lax.fori_loop(0, n_groups, body, 0)


def _t5_bucket(rel):
    nb = N_BUCKETS // 2
    max_exact = nb // 2
    ret = jnp.where(rel > 0, nb, 0)
    n = jnp.abs(rel)
    nf = jnp.maximum(n, 1).astype(jnp.float32)
    large = max_exact + (jnp.log(nf / max_exact) / math.log(MAX_DISTANCE / max_exact)
                         * (nb - max_exact)).astype(jnp.int32)
    large = jnp.minimum(large, nb - 1)
    return ret + jnp.where(n < max_exact, n, large)


def _const_spec(shape):
    zeros = (0,) * len(shape)
    return pl.BlockSpec(shape, lambda b, i: zeros, pipeline_mode=pl.Buffered(1))


def _layer_slice_spec(stacked, layer):
    return pl.BlockSpec((None,) + stacked.shape[1:], lambda b, i: (layer, 0, 0),
                        pipeline_mode=pl.Buffered(1))


def _layer_call(layer, x, vecs, win, wcat, bsp, cw, sink, biasm, ones, wout):
    first_layer = layer == 0
    bsz, seq, d = x.shape
    tm = TILE_ROWS
    assert seq % tm == 0 and tm % GROUP_ROWS == 0 and seq // GROUP_ROWS >= 2
    groups_per_tile = tm // GROUP_ROWS
    nblk_seq = seq // BLOCK
    grid = (bsz, seq // tm)

    in_specs = [
        pl.BlockSpec((None, tm, d), lambda b, i: (b, i, 0)),
        pl.BlockSpec((None, GROUP_ROWS, d),
                     lambda b, i: (b, jnp.minimum((i + 1) * groups_per_tile,
                                                  seq // GROUP_ROWS - 1), 0)),
        _layer_slice_spec(vecs, layer), _layer_slice_spec(win, layer),
        _layer_slice_spec(wcat, layer), _layer_slice_spec(bsp, layer), _layer_slice_spec(cw, layer),
        pl.BlockSpec(memory_space=pltpu.SMEM),
        _const_spec(biasm.shape), _const_spec(ones.shape), _layer_slice_spec(wout, layer),
    ]
    scratch = [
        pltpu.VMEM((GROUP_ROWS, d), _BF16),
        pltpu.VMEM((H_SLOTS, GROUP_ROWS if first_layer else SUBLANES, d), _F32),
        pltpu.VMEM((H_SLOTS, GROUP_ROWS, 3 * W_A), _F32),
        pltpu.VMEM((H_SLOTS, GROUP_ROWS, W_B), _F32),
        pltpu.VMEM((H_SLOTS, GROUP_ROWS, W_C), _BF16),
        pltpu.VMEM((H_SLOTS, GROUP_ROWS, W_C), _F32),
        pltpu.VMEM((BAND_SLOTS, 8, GROUP_ROWS, LANES), _BF16),
        pltpu.VMEM((BAND_SLOTS, GROUP_ROWS, W_B), _F32),
    ]
    return pl.pallas_call(
        functools.partial(_layer_kernel, layer, tm, nblk_seq),
        grid=grid,
        in_specs=in_specs,
        out_specs=pl.BlockSpec((None, tm, d), lambda b, i: (b, i, 0)),
        out_shape=jax.ShapeDtypeStruct(x.shape, x.dtype),
        scratch_shapes=scratch,
        compiler_params=pltpu.CompilerParams(
            dimension_semantics=("arbitrary", "arbitrary"),
            vmem_limit_bytes=VMEM_LIMIT_BYTES),
        name="layer_first" if first_layer else "layer_next",
    )(x, x, vecs, win, wcat, bsp, cw, sink, biasm, ones, wout)


def kernel(x, ln_in_g, ln_in_b, w_in, gmlp_ln_g, gmlp_ln_b, w_spatial, b_spatial, conv_w, conv_b,
           conv_ln_g, conv_ln_b, attn_sink, rel_bias, w_out, post_ln_g, post_ln_b):
    depth = w_in.shape[0]
    assert depth == DEPTH and x.shape[2] == D_MODEL and w_in.shape[2] == D_IN

    qq = jnp.arange(BLOCK)[:, None]
    kk = jnp.arange(3 * BLOCK)[None, :]
    bucket = _t5_bucket(kk - BLOCK - qq).astype(jnp.int32)
    biasm = pl.pallas_call(
        _bias_kernel,
        in_specs=[pl.BlockSpec(memory_space=pltpu.VMEM), pl.BlockSpec(memory_space=pltpu.SMEM)],
        out_specs=pl.BlockSpec(memory_space=pltpu.VMEM),
        out_shape=jax.ShapeDtypeStruct((3, N_Q_HEADS, BLOCK, 3 * BLOCK), _F32),
        name="rel_bias_tables",
    )(bucket, rel_bias.astype(_F32))

    kidx = jnp.arange(6 * BLOCK)[:, None] // (3 * BLOCK)
    ones = (kidx == jnp.arange(LANES)[None, :] // HEAD_DIM).astype(_BF16)

    def vec_rows(*vectors):
        rows = [jnp.pad(v.astype(_F32), ((0, 0), (0, D_MODEL - v.shape[-1]))) for v in vectors]
        rows.append(jnp.zeros((depth, D_MODEL * (VEC_ROWS - len(rows))), _F32))
        return jnp.concatenate(rows, axis=1).reshape(depth, VEC_ROWS, D_MODEL)

    tile_l = lambda v: jnp.broadcast_to(v[None, :], (depth, v.shape[0]))
    vecs = vec_rows(tile_l(ln_in_g), tile_l(ln_in_b), post_ln_g, post_ln_b, gmlp_ln_g, gmlp_ln_b,
                    conv_b, conv_ln_g, conv_ln_b)
    win_all = w_in.astype(_BF16)
    wout_all = w_out.astype(_BF16)
    wcat = jnp.transpose(w_spatial, (0, 2, 1, 3)).reshape(depth, BLOCK, N_HEADS_A * BLOCK).astype(_BF16)
    bsp = jnp.repeat(jnp.transpose(b_spatial, (0, 2, 1)), HEAD_DIM, axis=2).astype(_F32)
    cw = conv_w.astype(_F32)
    sink = attn_sink.astype(_F32)
    for l in range(depth):
        x = _layer_call(l, x, vecs, win_all, wcat, bsp, cw, sink, biasm, ones, wout_all)
    return x
```

```python
import functools
import math

import jax
import jax.numpy as jnp
from jax import lax
from jax.experimental import pallas as pl
from jax.experimental.pallas import tpu as pltpu

D_MODEL = 1024
HEAD_DIM = 64
W_A = 256
W_B = 256
W_C = 512
N_HEADS_A = 4
N_Q_HEADS = 8
N_KV_HEADS = 2
KV_W = N_KV_HEADS * HEAD_DIM
CONV_WIDTH = 31
CONV_PAD = CONV_WIDTH // 2
BLOCK = 128
N_BUCKETS = 32
MAX_DISTANCE = 128
LN_EPS = 1e-5
NEG_INF = -1e30
DEPTH = 2
DEEPNORM_ALPHA = (2 * DEPTH) ** 0.25
LOG2_E = math.log2(math.e)
Q_SCALE = HEAD_DIM ** -0.5 * LOG2_E

COL_A = 0
COL_B = 3 * W_A
COL_Q = COL_B + 3 * W_B
COL_KV = COL_Q + W_C
COL_CG = COL_KV + 2 * KV_W
D_IN = COL_CG + W_C

LANES = 128
SUBLANES = 8
CONV_HALO = 16
CONV_ROWS = 64
GROUP_BLOCKS = 4
GROUP_ROWS = GROUP_BLOCKS * BLOCK
H_SLOTS = 2
BAND_SLOTS = 4
TILE_ROWS = GROUP_ROWS
VMEM_LIMIT_BYTES = 60 * 1024 * 1024

(V_LN_IN_G, V_LN_IN_B, V_POST_LN_G, V_POST_LN_B, V_GMLP_LN_G, V_GMLP_LN_B,
 V_CONV_B, V_CONV_LN_G, V_CONV_LN_B) = range(9)
VEC_ROWS = 16

_BF16 = jnp.bfloat16
_F32 = jnp.float32


def _layer_norm(x, g, b):
    mu = jnp.mean(x, axis=-1, keepdims=True)
    xc = x - mu
    var = jnp.mean(xc * xc, axis=-1, keepdims=True)
    return xc * lax.rsqrt(var + LN_EPS) * g + b


def _dot(a, b):
    return jnp.dot(a, b, preferred_element_type=_F32)


def _bias_kernel(bucket_ref, rb_ref, out_ref):
    bucket = bucket_ref[...]
    row = lax.broadcasted_iota(jnp.int32, (BLOCK, 3 * BLOCK), 0)
    col = lax.broadcasted_iota(jnp.int32, (BLOCK, 3 * BLOCK), 1)
    in_window = jnp.abs(col - BLOCK - row) <= BLOCK
    for h in range(N_Q_HEADS):
        acc = jnp.zeros((BLOCK, 3 * BLOCK), _F32)
        for b in range(N_BUCKETS):
            acc = jnp.where(bucket == b, rb_ref[b, h] * LOG2_E, acc)
        base = jnp.where(in_window, acc, NEG_INF)
        out_ref[0, h] = base
        out_ref[1, h] = jnp.where(col < BLOCK, NEG_INF, base)
        out_ref[2, h] = jnp.where(col >= 2 * BLOCK, NEG_INF, base)


def _layer_kernel(layer, tm, nblk_seq,
                  xc_ref, xn_ref, vec_ref, win_ref, wcat_ref, bsp_ref, cw_ref,
                  sink_ref, biasm_ref, ones_ref, wout_ref,
                  out_ref,
                  xb_s, xres_r, ha_r, bg_r, q_r, cg_r, kvar_r, y_r):
    first_layer = layer == 0
    i = pl.program_id(1)
    n_groups = tm // GROUP_ROWS
    n_groups_seq = nblk_seq // GROUP_BLOCKS
    low = lax.broadcasted_iota(jnp.int32, (1, LANES), 1) < HEAD_DIM
    head_of_lane = lax.broadcasted_iota(jnp.int32, (1, W_A), 1) // HEAD_DIM

    def vec(row, width=D_MODEL):
        return vec_ref[row:row + 1, 0:width]

    def project_pieces(x_grp, p_new):
        hs = p_new & (H_SLOTS - 1)
        bs = p_new & (BAND_SLOTS - 1)
        if first_layer:
            x_grp = _layer_norm(x_grp, vec(V_LN_IN_G), vec(V_LN_IN_B))
            xres_r[hs] = x_grp
        xb_s[...] = x_grp.astype(_BF16)

        def piece_a():
            ha_r[hs] = _dot(xb_s[...], win_ref[:, COL_A:COL_B])

        def piece_b():
            hb = _dot(xb_s[...], win_ref[:, COL_B:COL_Q])
            bg_r[hs] = hb[:, 2 * W_B:3 * W_B]
            y = hb[:, 0:W_B] * jax.nn.sigmoid(hb[:, W_B:2 * W_B])
            y_r[bs] = jnp.where(p_new < n_groups_seq, y, 0.0)

        def piece_q():
            q_r[hs] = (_dot(xb_s[...], win_ref[:, COL_Q:COL_KV]) * Q_SCALE).astype(_BF16)

        def piece_kv():
            kv = _dot(xb_s[...], win_ref[:, COL_KV:COL_CG])
            zero = jnp.zeros((GROUP_ROWS, LANES), _BF16)
            for t in range(2):
                nat = kv[:, t * KV_W:(t + 1) * KV_W]
                swp = pltpu.roll(nat, HEAD_DIM, axis=1).astype(_BF16)
                nat = nat.astype(_BF16)
                kvar_r[bs, 4 * t + 0] = jnp.where(low, nat, zero)
                kvar_r[bs, 4 * t + 1] = jnp.where(low, zero, swp)
                kvar_r[bs, 4 * t + 2] = jnp.where(low, swp, zero)
                kvar_r[bs, 4 * t + 3] = jnp.where(low, zero, nat)

        def piece_cg():
            cg_r[hs] = _dot(xb_s[...], win_ref[:, COL_CG:D_IN])

        return [piece_kv, piece_b, piece_q, piece_cg, piece_a]

    def mix_block(b, p, issue):
        hs = p & (H_SLOTS - 1)
        s_here = p & (BAND_SLOTS - 1)
        blk = [slice(k * BLOCK, (k + 1) * BLOCK) for k in range(GROUP_BLOCKS)]
        here = blk[b]
        before = ((p + BAND_SLOTS - 1) & (BAND_SLOTS - 1), blk[-1]) if b == 0 else (s_here, blk[b - 1])
        after = ((p + 1) & (BAND_SLOTS - 1), blk[0]) if b == GROUP_BLOCKS - 1 else (s_here, blk[b + 1])
        band_at = [before, (s_here, here), after]
        g = GROUP_BLOCKS * p + b

        u = jax.nn.gelu(ha_r[hs, here, 0:W_A])
        v = _layer_norm(jax.nn.gelu(ha_r[hs, here, W_A:2 * W_A]),
                        vec(V_GMLP_LN_G, W_A), vec(V_GMLP_LN_B, W_A))
        gate_a = jax.nn.silu(ha_r[hs, here, 2 * W_A:3 * W_A])
        vb = v.astype(_BF16)
        zero = jnp.zeros_like(vb)
        rhs = jnp.concatenate([jnp.where(head_of_lane == h, vb, zero)
                               for h in range(N_HEADS_A)], axis=0)
        sp = _dot(wcat_ref[...], rhs) + bsp_ref[...]
        ya = (u * sp * gate_a).astype(_BF16)

        (sp_, rp_), (sc_, rc_), (sn_, rn_) = band_at
        ywin = jnp.concatenate([y_r[sp_, rp_.stop - CONV_HALO:rp_.stop, :], y_r[sc_, rc_, :],
                                y_r[sn_, rn_.start:rn_.start + CONV_HALO, :]], axis=0)
        span = CONV_ROWS + 2 * CONV_HALO
        yb_parts = []
        for ci in range(BLOCK // CONV_ROWS):
            issue()
            base = ci * CONV_ROWS
            ych = ywin[base:base + span, :]
            acc = jnp.zeros((CONV_ROWS, W_B), _F32)
            for r in range(SUBLANES):
                zr = ych if r == 0 else pltpu.roll(ych, span - r, axis=0)
                for m in range(2 * CONV_HALO // SUBLANES):
                    k = SUBLANES * m + r - 1
                    if 0 <= k < CONV_WIDTH:
                        acc = acc + zr[SUBLANES * m:SUBLANES * m + CONV_ROWS, :] * cw_ref[k:k + 1, :]
            z = _layer_norm(acc + vec(V_CONV_B, W_B), vec(V_CONV_LN_G, W_B), vec(V_CONV_LN_B, W_B))
            gate_b = jax.nn.silu(bg_r[hs, here.start + base:here.start + base + CONV_ROWS, :])
            yb_parts.append((jax.nn.silu(z) * gate_b).astype(_BF16))
        yb = jnp.concatenate(yb_parts, axis=0)

        edge = jnp.where(g == 0, 1, jnp.where(g == nblk_seq - 1, 2, 0))

        def band(var):
            return [kvar_r[s_, var, r_, :] for (s_, r_) in band_at]

        yc_parts = []
        for kvh in range(N_KV_HEADS):
            issue()
            lhs = jnp.concatenate([q_r[hs, here, (2 * kvh + pp) * LANES:(2 * kvh + pp + 1) * LANES]
                                   for pp in range(2)], axis=0)
            kcat = jnp.concatenate(band(2 * kvh) + band(2 * kvh + 1), axis=0)
            s2 = lax.dot_general(lhs, kcat, (((1,), (1,)), ((), ())),
                                 preferred_element_type=_F32)
            p_rows = []
            sink_w = []
            for pp in range(2):
                p_pair = []
                for hh in range(2):
                    h = 4 * kvh + 2 * pp + hh
                    s = (s2[pp * BLOCK:(pp + 1) * BLOCK, hh * 3 * BLOCK:(hh + 1) * 3 * BLOCK]
                         + biasm_ref[edge, h])
                    sk = sink_ref[layer, h] * LOG2_E
                    m = jnp.maximum(jnp.max(s, axis=-1, keepdims=True), sk)
                    p_pair.append(jnp.exp2((s - m).astype(_BF16)))
                    sink_w.append(jnp.exp2(sk - m))
                p_rows.append(jnp.concatenate(p_pair, axis=1))
            pcat = jnp.concatenate(p_rows, axis=0)
            vcat = jnp.concatenate(
                [jnp.concatenate(band(4 + 2 * kvh) + band(5 + 2 * kvh), axis=0), ones_ref[...]], axis=1)
            o2 = _dot(pcat, vcat)
            for pp in range(2):
                rsl = slice(pp * BLOCK, (pp + 1) * BLOCK)
                cols = slice((2 * kvh + pp) * LANES, (2 * kvh + pp + 1) * LANES)
                den = o2[rsl, LANES:2 * LANES] + jnp.where(low, sink_w[2 * pp], sink_w[2 * pp + 1])
                o = o2[rsl, 0:LANES] / den
                yc_parts.append((o * jax.nn.silu(cg_r[hs, here, cols])).astype(_BF16))
        return jnp.concatenate([ya, yb] + yc_parts, axis=1)

    def mix(jg, p, pieces):
        pieces = list(pieces)

        def issue(n=1):
            for _ in range(n):
                if pieces:
                    pieces.pop(0)()
        rows = pl.ds(pl.multiple_of(jg * GROUP_ROWS, GROUP_ROWS), GROUP_ROWS)
        hs = p & (H_SLOTS - 1)
        issue(2)
        ymix = jnp.concatenate([mix_block(b, p, issue) for b in range(GROUP_BLOCKS)], axis=0)

        issue(len(pieces))
        y_out = _dot(ymix, wout_ref[...])
        x_res = xres_r[hs] if first_layer else xc_ref[rows, :]
        out_ref[rows, :] = _layer_norm(DEEPNORM_ALPHA * x_res + y_out, vec(V_POST_LN_G), vec(V_POST_LN_B))

    p0 = i * n_groups

    @pl.when(i == 0)
    def _():
        kvar_r[BAND_SLOTS - 1] = jnp.zeros(kvar_r.shape[1:], _BF16)
        y_r[BAND_SLOTS - 1] = jnp.zeros(y_r.shape[1:], _F32)
        for piece in project_pieces(xc_ref[0:GROUP_ROWS, :], p0):
            piece()

    def body(jg, carry):
        p = p0 + jg
        if n_groups == 1:
            x_next = xn_ref[...]
        else:
            nxt = pl.multiple_of(jnp.minimum(jg + 1, n_groups - 1) * GROUP_ROWS, GROUP_ROWS)
            x_next = jnp.where(jg + 1 < n_groups, xc_ref[pl.ds(nxt, GROUP_ROWS), :], xn_ref[...])
        mix(jg, p, project_pieces(x_next, p + 1))
        return carry

    lax.fori_loop(0, n_groups, body, 0)


def _t5_bucket(rel):
    nb = N_BUCKETS // 2
    max_exact = nb // 2
    ret = jnp.where(rel > 0, nb, 0)
    n = jnp.abs(rel)
    nf = jnp.maximum(n, 1).astype(jnp.float32)
    large = max_exact + (jnp.log(nf / max_exact) / math.log(MAX_DISTANCE / max_exact)
                         * (nb - max_exact)).astype(jnp.int32)
    large = jnp.minimum(large, nb - 1)
    return ret + jnp.where(n < max_exact, n, large)


def _const_spec(shape):
    zeros = (0,) * len(shape)
    return pl.BlockSpec(shape, lambda b, i: zeros, pipeline_mode=pl.Buffered(1))


def _layer_slice_spec(stacked, layer):
    return pl.BlockSpec((None,) + stacked.shape[1:], lambda b, i: (layer, 0, 0),
                        pipeline_mode=pl.Buffered(1))


def _layer_call(layer, x, vecs, win, wcat, bsp, cw, sink, biasm, ones, wout):
    first_layer = layer == 0
    bsz, seq, d = x.shape
    tm = TILE_ROWS
    assert seq % tm == 0 and tm % GROUP_ROWS == 0 and seq // GROUP_ROWS >= 2
    groups_per_tile = tm // GROUP_ROWS
    nblk_seq = seq // BLOCK
    grid = (bsz, seq // tm)

    in_specs = [
        pl.BlockSpec((None, tm, d), lambda b, i: (b, i, 0)),
        pl.BlockSpec((None, GROUP_ROWS, d),
                     lambda b, i: (b, jnp.minimum((i + 1) * groups_per_tile,
                                                  seq // GROUP_ROWS - 1), 0)),
        _layer_slice_spec(vecs, layer), _layer_slice_spec(win, layer),
        _layer_slice_spec(wcat, layer), _layer_slice_spec(bsp, layer), _layer_slice_spec(cw, layer),
        pl.BlockSpec(memory_space=pltpu.SMEM),
        _const_spec(biasm.shape), _const_spec(ones.shape), _layer_slice_spec(wout, layer),
    ]
    scratch = [
        pltpu.VMEM((GROUP_ROWS, d), _BF16),
        pltpu.VMEM((H_SLOTS, GROUP_ROWS if first_layer else SUBLANES, d), _F32),
        pltpu.VMEM((H_SLOTS, GROUP_ROWS, 3 * W_A), _F32),
        pltpu.VMEM((H_SLOTS, GROUP_ROWS, W_B), _F32),
        pltpu.VMEM((H_SLOTS, GROUP_ROWS, W_C), _BF16),
        pltpu.VMEM((H_SLOTS, GROUP_ROWS, W_C), _F32),
        pltpu.VMEM((BAND_SLOTS, 8, GROUP_ROWS, LANES), _BF16),
        pltpu.VMEM((BAND_SLOTS, GROUP_ROWS, W_B), _F32),
    ]
    return pl.pallas_call(
        functools.partial(_layer_kernel, layer, tm, nblk_seq),
        grid=grid,
        in_specs=in_specs,
        out_specs=pl.BlockSpec((None, tm, d), lambda b, i: (b, i, 0)),
        out_shape=jax.ShapeDtypeStruct(x.shape, x.dtype),
        scratch_shapes=scratch,
        compiler_params=pltpu.CompilerParams(
            dimension_semantics=("arbitrary", "arbitrary"),
            vmem_limit_bytes=VMEM_LIMIT_BYTES),
        name="layer_first" if first_layer else "layer_next",
    )(x, x, vecs, win, wcat, bsp, cw, sink, biasm, ones, wout)


def kernel(x, ln_in_g, ln_in_b, w_in, gmlp_ln_g, gmlp_ln_b, w_spatial, b_spatial, conv_w, conv_b,
           conv_ln_g, conv_ln_b, attn_sink, rel_bias, w_out, post_ln_g, post_ln_b):
    depth = w_in.shape[0]
    assert depth == DEPTH and x.shape[2] == D_MODEL and w_in.shape[2] == D_IN

    qq = jnp.arange(BLOCK)[:, None]
    kk = jnp.arange(3 * BLOCK)[None, :]
    bucket = _t5_bucket(kk - BLOCK - qq).astype(jnp.int32)
    biasm = pl.pallas_call(
        _bias_kernel,
        in_specs=[pl.BlockSpec(memory_space=pltpu.VMEM), pl.BlockSpec(memory_space=pltpu.SMEM)],
        out_specs=pl.BlockSpec(memory_space=pltpu.VMEM),
        out_shape=jax.ShapeDtypeStruct((3, N_Q_HEADS, BLOCK, 3 * BLOCK), _F32),
        name="rel_bias_tables",
    )(bucket, rel_bias.astype(_F32))

    kidx = jnp.arange(6 * BLOCK)[:, None] // (3 * BLOCK)
    ones = (kidx == jnp.arange(LANES)[None, :] // HEAD_DIM).astype(_BF16)

    def vec_rows(*vectors):
        rows = [jnp.pad(v.astype(_F32), ((0, 0), (0, D_MODEL - v.shape[-1]))) for v in vectors]
        rows.append(jnp.zeros((depth, D_MODEL * (VEC_ROWS - len(rows))), _F32))
        return jnp.concatenate(rows, axis=1).reshape(depth, VEC_ROWS, D_MODEL)

    tile_l = lambda v: jnp.broadcast_to(v[None, :], (depth, v.shape[0]))
    vecs = vec_rows(tile_l(ln_in_g), tile_l(ln_in_b), post_ln_g, post_ln_b, gmlp_ln_g, gmlp_ln_b,
                    conv_b, conv_ln_g, conv_ln_b)
    win_all = w_in.astype(_BF16)
    wout_all = w_out.astype(_BF16)
    wcat = jnp.transpose(w_spatial, (0, 2, 1, 3)).reshape(depth, BLOCK, N_HEADS_A * BLOCK).astype(_BF16)
    bsp = jnp.repeat(jnp.transpose(b_spatial, (0, 2, 1)), HEAD_DIM, axis=2).astype(_F32)
    cw = conv_w.astype(_F32)
    sink = attn_sink.astype(_F32)
    for l in range(depth):
        x = _layer_call(l, x, vecs, win_all, wcat, bsp, cw, sink, biasm, ones, wout_all)
    return x
```
